```python
import math
import jax, jax.numpy as jnp
from jax import lax
import numpy as np

D_MODEL = 1024
BATCH = 8
SEQ = 2048
DEPTH = 2
DEC_BATCH = 128
DEC_SEQ = 1
PAST_LEN = 2048
PAGE_SIZE = 128

N_A_LAYERS = DEPTH // 2
N_B_LAYERS = DEPTH - N_A_LAYERS
CHUNK = 128
D_GM = 3 * D_MODEL
N_SGU_GROUPS = 8
SGU_GROUP = D_GM // N_SGU_GROUPS
N_HEADS = 8
HEAD_DIM = D_MODEL // (2 * N_HEADS)
V_DIM = 2 * HEAD_DIM
Q_BLOCK = 128
N_EXPERT_GROUPS = 4
EXPERTS_PER_GROUP = 4
N_EXPERTS = N_EXPERT_GROUPS * EXPERTS_PER_GROUP
TOP_K_INNER = 2
D_EXPERT = 256
RMS_EPS = 1e-6
MASK_VALUE = -1e30

kernel_name = 'yoco_gmlp_diffattn_hmoe_step'

F32 = jnp.float32


def rms_norm(x, g):
    xf = x.astype(F32)
    y = xf * lax.rsqrt(jnp.mean(xf * xf, axis=-1, keepdims=True) + RMS_EPS)
    return (y * g.astype(F32)).astype(x.dtype)


def chunk_spatial_gate(v, w_s, b_s):
    b, t, _ = v.shape
    pad = (-t) % CHUNK
    vp = jnp.pad(v, ((0, 0), (0, pad), (0, 0)))
    nc = (t + pad) // CHUNK
    vp = vp.reshape(b, nc, CHUNK, N_SGU_GROUPS, SGU_GROUP)
    causal = jnp.tril(jnp.ones((CHUNK, CHUNK), dtype=bool))
    w = jnp.where(causal[None], w_s, jnp.zeros_like(w_s)).astype(v.dtype)
    out = jnp.einsum('gts,bcsgk->bctgk', w, vp) + b_s.T[None, None, :, :, None].astype(v.dtype)
    return out.reshape(b, nc * CHUNK, D_GM)[:, :t]


def gmlp_mixer(h, w_uv, sgu_gain, w_s, b_s, w_o):
    z = jax.nn.gelu(h @ w_uv)
    u, v = jnp.split(z, 2, axis=-1)
    v = rms_norm(v, sgu_gain)
    return (u * chunk_spatial_gate(v, w_s, b_s)) @ w_o, v


def hierarchical_moe(h, w_rg, b_rg, w_re, b_re, w_gate, w_up, w_down):
    shape = h.shape
    hf = h.reshape(-1, shape[-1])
    g_prob = jax.nn.softmax((hf @ w_rg + b_rg).astype(F32), axis=-1)
    p_grp, grp = lax.top_k(g_prob, 1)
    e_logits = (jnp.einsum('td,gde->tge', hf, w_re) + b_re).astype(F32)
    e_logits = jnp.take_along_axis(e_logits, grp[:, :, None], axis=1)[:, 0]
    e_val, e_idx = lax.top_k(e_logits, TOP_K_INNER)
    e_w = jax.nn.softmax(e_val, axis=-1) * p_grp
    inner = jnp.sum(jax.nn.one_hot(e_idx, EXPERTS_PER_GROUP, dtype=F32) * e_w[..., None], axis=1)
    gate = (jax.nn.one_hot(grp[:, 0], N_EXPERT_GROUPS, dtype=F32)[:, :, None]
            * inner[:, None, :]).reshape(-1, N_EXPERTS)
    hid = jax.nn.silu(jnp.einsum('td,edf->tef', hf, w_gate)) * jnp.einsum('td,edf->tef', hf, w_up)
    y = jnp.einsum('tef,efd->td', hid * gate[..., None].astype(hid.dtype), w_down)
    return y.reshape(shape)


def diff_attention(q, k_parts, v_parts, q_pos, k_pos, lam, slopes):
    qf = q.astype(F32) * (HEAD_DIM ** -0.5)
    s = jnp.concatenate([jnp.einsum('bqhmd,bkhmd->bmhqk', qf, k.astype(F32)) for k in k_parts], axis=-1)
    dist = (q_pos[:, None] - k_pos[None, :]).astype(F32)
    bias = jnp.where(dist >= 0, -slopes[:, None, None] * dist, MASK_VALUE)
    p = jax.nn.softmax(s + bias, axis=-1)
    a = p[:, 0] - lam * p[:, 1]
    splits = [int(c) for c in np.cumsum([k.shape[1] for k in k_parts])[:-1]]
    a_parts = jnp.split(a, splits, axis=-1)
    o = jnp.einsum('bhqk,bkhv->bqhv', a_parts[0], v_parts[0].astype(F32))
    for a_i, v_i in zip(a_parts[1:], v_parts[1:]):
        o = o + jnp.einsum('bhqk,bkhv->bqhv', a_i, v_i.astype(F32))
    return o


def channel_mixer(x, i, p):
    return hierarchical_moe(rms_norm(x, p['norm_ffn'][i]), p['w_router_group'][i], p['b_router_group'][i],
                            p['w_router_expert'][i], p['b_router_expert'][i],
                            p['w_e_gate'][i], p['w_e_up'][i], p['w_e_down'][i])


def a_layers(x, p):
    v_rows = []
    for i in range(N_A_LAYERS):
        h = rms_norm(x, p['norm_mix'][i])
        o, v = gmlp_mixer(h, p['w_uv'][i], p['sgu_gain'][i], p['w_sgu'][i], p['b_sgu'][i], p['w_gmlp_o'][i])
        x = x + o
        x = x + channel_mixer(x, i, p)
        v_rows.append(v)
    return x, v_rows


def kv_project(x, p):
    b, t, _ = x.shape
    hk = rms_norm(x, p['norm_kv'])
    k = (hk @ p['w_k']).reshape(b, t, N_HEADS, 2, HEAD_DIM)
    v = (hk @ p['w_v']).reshape(b, t, N_HEADS, V_DIM)
    return k, v


def b_layers(x, attend, p):
    for j in range(N_B_LAYERS):
        i = N_A_LAYERS + j
        lam_init = 0.8 - 0.6 * math.exp(-0.3 * i)
        h = rms_norm(x, p['norm_mix'][i])
        b, t, _ = h.shape
        q = (h @ p['w_q'][j]).reshape(b, t, N_HEADS, 2, HEAD_DIM)
        lq = p['lam_q'][j].astype(F32)
        lk = p['lam_k'][j].astype(F32)
        lam = jnp.exp(jnp.sum(lq[0] * lk[0])) - jnp.exp(jnp.sum(lq[1] * lk[1])) + lam_init
        o = attend(q, lam)
        o = rms_norm(o, p['subln_gain'][j]) * (1.0 - lam_init)
        x = x + o.reshape(b, t, N_HEADS * V_DIM).astype(x.dtype) @ p['w_attn_o'][j]
        x = x + channel_mixer(x, i, p)
    return x


def setup_inputs(seed: int = 0) -> dict:
    key = jax.random.key(seed)
    ks = jax.random.split(key, 32)
    n_pages = PAST_LEN // PAGE_SIZE
    n_used = DEC_BATCH * n_pages
    n_pool = n_used + max(1, n_used // 4)

    def nrm(k, shape, scale):
        return jax.random.normal(k, shape, F32) * scale

    page_table = jax.random.permutation(ks[4], n_pool)[:n_used].reshape(DEC_BATCH, n_pages).astype(jnp.int32)
    return {
        'x_prompt': nrm(ks[0], (BATCH, SEQ, D_MODEL), 1.0),
        'x_sample': nrm(ks[1], (DEC_BATCH, DEC_SEQ, D_MODEL), 1.0),
        'cache_k': nrm(ks[2], (n_pool, PAGE_SIZE, N_HEADS, 2, HEAD_DIM), 1.0),
        'cache_v': nrm(ks[3], (n_pool, PAGE_SIZE, N_HEADS, V_DIM), 1.0),
        'page_table': page_table,
        'norm_mix': 1.0 + nrm(ks[5], (DEPTH, D_MODEL), 0.02),
        'norm_ffn': 1.0 + nrm(ks[6], (DEPTH, D_MODEL), 0.02),
        'w_uv': nrm(ks[7], (N_A_LAYERS, D_MODEL, 2 * D_GM), D_MODEL ** -0.5),
        'sgu_gain': 1.0 + nrm(ks[8], (N_A_LAYERS, D_GM), 0.02),
        'w_sgu': nrm(ks[9], (N_A_LAYERS, N_SGU_GROUPS, CHUNK, CHUNK), CHUNK ** -0.5),
        'b_sgu': 1.0 + nrm(ks[10], (N_A_LAYERS, N_SGU_GROUPS, CHUNK), 0.1),
        'w_gmlp_o': nrm(ks[11], (N_A_LAYERS, D_GM, D_MODEL), D_GM ** -0.5),
        'norm_kv': 1.0 + nrm(ks[12], (D_MODEL,), 0.02),
        'w_k': nrm(ks[13], (D_MODEL, N_HEADS * 2 * HEAD_DIM), D_MODEL ** -0.5),
        'w_v': nrm(ks[14], (D_MODEL, N_HEADS * V_DIM), D_MODEL ** -0.5),
        'w_q': nrm(ks[15], (N_B_LAYERS, D_MODEL, N_HEADS * 2 * HEAD_DIM), D_MODEL ** -0.5),
        'lam_q': nrm(ks[16], (N_B_LAYERS, 2, HEAD_DIM), 0.1),
        'lam_k': nrm(ks[17], (N_B_LAYERS, 2, HEAD_DIM), 0.1),
        'subln_gain': 1.0 + nrm(ks[18], (N_B_LAYERS, V_DIM), 0.02),
        'w_attn_o': nrm(ks[19], (N_B_LAYERS, N_HEADS * V_DIM, D_MODEL), (N_HEADS * V_DIM) ** -0.5),
        'w_router_group': nrm(ks[20], (DEPTH, D_MODEL, N_EXPERT_GROUPS), D_MODEL ** -0.5),
        'b_router_group': nrm(ks[21], (DEPTH, N_EXPERT_GROUPS), 0.01),
        'w_router_expert': nrm(ks[22], (DEPTH, N_EXPERT_GROUPS, D_MODEL, EXPERTS_PER_GROUP), D_MODEL ** -0.5),
        'b_router_expert': nrm(ks[23], (DEPTH, N_EXPERT_GROUPS, EXPERTS_PER_GROUP), 0.01),
        'w_e_gate': nrm(ks[24], (DEPTH, N_EXPERTS, D_MODEL, D_EXPERT), D_MODEL ** -0.5),
        'w_e_up': nrm(ks[25], (DEPTH, N_EXPERTS, D_MODEL, D_EXPERT), D_MODEL ** -0.5),
        'w_e_down': nrm(ks[26], (DEPTH, N_EXPERTS, D_EXPERT, D_MODEL), D_EXPERT ** -0.5),
        'norm_final': 1.0 + nrm(ks[27], (D_MODEL,), 0.02),
    }


def reference(x_prompt, x_sample, cache_k, cache_v, page_table, norm_mix, norm_ffn, w_uv, sgu_gain,
              w_sgu, b_sgu, w_gmlp_o, norm_kv, w_k, w_v, w_q, lam_q, lam_k, subln_gain, w_attn_o,
              w_router_group, b_router_group, w_router_expert, b_router_expert, w_e_gate, w_e_up,
              w_e_down, norm_final):
    p = {'norm_mix': norm_mix, 'norm_ffn': norm_ffn, 'w_uv': w_uv, 'sgu_gain': sgu_gain, 'w_sgu': w_sgu,
         'b_sgu': b_sgu, 'w_gmlp_o': w_gmlp_o, 'norm_kv': norm_kv, 'w_k': w_k, 'w_v': w_v, 'w_q': w_q,
         'lam_q': lam_q, 'lam_k': lam_k, 'subln_gain': subln_gain, 'w_attn_o': w_attn_o,
         'w_router_group': w_router_group, 'b_router_group': b_router_group,
         'w_router_expert': w_router_expert, 'b_router_expert': b_router_expert,
         'w_e_gate': w_e_gate, 'w_e_up': w_e_up, 'w_e_down': w_e_down}
    slopes = 2.0 ** (-8.0 * jnp.arange(1, N_HEADS + 1, dtype=F32) / N_HEADS)

    bsz, seq, _ = x_prompt.shape
    xp, _ = a_layers(x_prompt, p)
    k_p, v_p = kv_project(xp, p)
    pos_p = jnp.arange(seq)
    nb = seq // Q_BLOCK

    def attend_prompt(q, lam):
        qb = jnp.moveaxis(q.reshape(bsz, nb, Q_BLOCK, N_HEADS, 2, HEAD_DIM), 1, 0)
        qpos = pos_p.reshape(nb, Q_BLOCK)

        def block(args):
            q_blk, qp = args
            return diff_attention(q_blk, (k_p,), (v_p,), qp, pos_p, lam, slopes)

        o = lax.map(block, (qb, qpos))
        return jnp.moveaxis(o, 0, 1).reshape(bsz, seq, N_HEADS, V_DIM)

    y_prompt = rms_norm(b_layers(xp, attend_prompt, p), norm_final)

    dec_b, dec_s, _ = x_sample.shape
    past_len = page_table.shape[1] * cache_k.shape[1]
    xs, gmlp_v_rows = a_layers(x_sample, p)
    k_s, v_s = kv_project(xs, p)
    k_past = cache_k[page_table].reshape(dec_b, past_len, N_HEADS, 2, HEAD_DIM)
    v_past = cache_v[page_table].reshape(dec_b, past_len, N_HEADS, V_DIM)
    q_pos_s = past_len + jnp.arange(dec_s)
    k_pos_s = jnp.arange(past_len + dec_s)

    def attend_sample(q, lam):
        return diff_attention(q, (k_past, k_s), (v_past, v_s), q_pos_s, k_pos_s, lam, slopes)

    y_sample = rms_norm(b_layers(xs, attend_sample, p), norm_final)

    n_pp = seq // PAGE_SIZE
    k_prompt = k_p.reshape(bsz, n_pp, PAGE_SIZE, N_HEADS, 2, HEAD_DIM)
    v_prompt = v_p.reshape(bsz, n_pp, PAGE_SIZE, N_HEADS, V_DIM)
    gmlp_v_sample = jnp.stack(gmlp_v_rows)
    return (y_prompt, y_sample, k_prompt, v_prompt, k_s, v_s, gmlp_v_sample)
```

```python
import functools
import math

import jax
import jax.numpy as jnp
from jax import lax
from jax.experimental import pallas as pl
from jax.experimental.pallas import tpu as pltpu

F32 = jnp.float32
BF16 = jnp.bfloat16

RMS_EPS = 1e-6
MASK_VALUE = -1e30
TOP_K_INNER = 2
LANES = 128
VMEM_LIMIT_BYTES = 56 * 1024 * 1024


def _rms(x, g):
    ms = jnp.mean(x * x, axis=-1, keepdims=True)
    return x * lax.rsqrt(ms + RMS_EPS) * g


def _gelu_tanh(x):
    c = math.sqrt(2.0 / math.pi)
    return x * (0.5 * (1.0 + jnp.tanh(c * (x + 0.044715 * (x * x * x)))))


def _dot(a, b):
    return jnp.dot(a, b, preferred_element_type=F32)


def _dot_nt(a, b):
    return lax.dot_general(a, b, (((1,), (1,)), ((), ())), preferred_element_type=F32)


def _params(sem):
    return pltpu.CompilerParams(dimension_semantics=sem, vmem_limit_bytes=VMEM_LIMIT_BYTES)


def _const_spec(shape):
    nd = len(shape)
    return pl.BlockSpec(shape, lambda *_: (0,) * nd)


def _gmlp_kernel(x_ref, nm_ref, wuv_ref, sg_ref, ws_ref, bs_ref, wo_ref, *out_refs,
                 seq_one, chunk, n_groups, slab_groups):
    if seq_one:
        o_ref, v_ref = out_refs
    else:
        (o_ref,) = out_refs
    tm, _ = x_ref.shape
    d_gm = sg_ref.shape[1]
    sgw = d_gm // n_groups
    x = x_ref[...]
    h = _rms(x, nm_ref[...]).astype(BF16)
    v = _gelu_tanh(_dot(h, wuv_ref[:, d_gm:]))
    vn = _rms(v, sg_ref[...])
    if seq_one:
        v_ref[...] = vn
        vb = vn.astype(BF16).astype(F32)
    else:
        vb = vn.astype(BF16)
        row = lax.broadcasted_iota(jnp.int32, (chunk, chunk), 0)
        col = lax.broadcasted_iota(jnp.int32, (chunk, chunk), 1)
        causal = row >= col
    slab = slab_groups * sgw
    acc = x
    for s in range(n_groups // slab_groups):
        c0 = s * slab
        u = _gelu_tanh(_dot(h, wuv_ref[:, c0:c0 + slab]))
        if seq_one:
            gate = vb[:, c0:c0 + slab] * ws_ref[:, c0:c0 + slab] + bs_ref[:, c0:c0 + slab]
        else:
            cols = []
            for gg in range(slab_groups):
                g = s * slab_groups + gg
                w = jnp.where(causal, ws_ref[g], 0.0).astype(BF16)
                bias = jnp.broadcast_to(bs_ref[:, g:g + 1], (chunk, sgw))
                rows = []
                for c in range(tm // chunk):
                    vv = vb[c * chunk:(c + 1) * chunk, g * sgw:(g + 1) * sgw]
                    rows.append(_dot(w, vv) + bias)
                cols.append(jnp.concatenate(rows, axis=0) if len(rows) > 1 else rows[0])
            gate = jnp.concatenate(cols, axis=1) if len(cols) > 1 else cols[0]
        p = (u * gate).astype(BF16)
        acc = acc + _dot(p, wo_ref[c0:c0 + slab, :])
    o_ref[...] = acc


def _gmlp(x, nm, wuv, sg, ws, bs, wo, *, seq_one, chunk, n_groups, tm):
    t, d = x.shape
    d_gm = sg.shape[1]
    tm = min(tm, t)
    slab_groups = 2 if n_groups % 2 == 0 else 1
    out_shape = [jax.ShapeDtypeStruct((t, d), F32)]
    out_specs = [pl.BlockSpec((tm, d), lambda i: (i, 0))]
    if seq_one:
        out_shape.append(jax.ShapeDtypeStruct((t, d_gm), F32))
        out_specs.append(pl.BlockSpec((tm, d_gm), lambda i: (i, 0)))
    kern = functools.partial(_gmlp_kernel, seq_one=seq_one, chunk=chunk, n_groups=n_groups,
                             slab_groups=slab_groups)
    return pl.pallas_call(
        kern,
        grid=(t // tm,),
        in_specs=[pl.BlockSpec((tm, d), lambda i: (i, 0)),
                  _const_spec(nm.shape), _const_spec(wuv.shape), _const_spec(sg.shape),
                  _const_spec(ws.shape), _const_spec(bs.shape), _const_spec(wo.shape)],
        out_specs=out_specs,
        out_shape=out_shape,
        compiler_params=_params(("arbitrary",)),
        name="gmlp_seq1" if seq_one else "gmlp",
    )(x, nm, wuv, sg, ws, bs, wo)


def _router_gate(hf, wrg, brg, wre, bre, n_groups, epg):
    tm = hf.shape[0]
    n_e = n_groups * epg
    lg = jnp.dot(hf, wrg, preferred_element_type=F32, precision=lax.Precision.HIGHEST) + brg
    lane_g = lax.broadcasted_iota(jnp.int32, (tm, n_groups), 1)
    mg = jnp.max(lg, axis=-1, keepdims=True)
    p_grp = 1.0 / jnp.sum(jnp.exp(lg - mg), axis=-1, keepdims=True)
    grp = jnp.min(jnp.where(lg == mg, lane_g, n_groups), axis=-1, keepdims=True)
    el = jnp.dot(hf, wre, preferred_element_type=F32, precision=lax.Precision.HIGHEST) + bre
    lane_e = lax.broadcasted_iota(jnp.int32, (tm, n_e), 1)
    neg = jnp.float32(-jnp.inf)
    in_grp = (lane_e >= grp * epg) & (lane_e < grp * epg + epg)
    el1 = jnp.where(in_grp, el, neg)
    v1 = jnp.max(el1, axis=-1, keepdims=True)
    i1 = jnp.min(jnp.where(el1 == v1, lane_e, n_e), axis=-1, keepdims=True)
    el2 = jnp.where(lane_e == i1, neg, el1)
    v2 = jnp.max(el2, axis=-1, keepdims=True)
    i2 = jnp.min(jnp.where(el2 == v2, lane_e, n_e), axis=-1, keepdims=True)
    e2 = jnp.exp(v2 - v1)
    w1 = 1.0 / (1.0 + e2)
    w2 = e2 / (1.0 + e2)
    gate = jnp.where(lane_e == i1, w1 * p_grp, 0.0) + jnp.where(lane_e == i2, w2 * p_grp, 0.0)
    return gate


def _moe_dense_kernel(x_ref, nf_ref, wrg_ref, brg_ref, wre_ref, bre_ref, wg_ref, wu_ref, wd_ref,
                      fin_ref, o_ref, hb_ref, gate_ref, *, n_groups, epg, final_norm):
    e = pl.program_id(1)
    n_e = n_groups * epg

    @pl.when(e == 0)
    def _():
        x = x_ref[...]
        hf = _rms(x, nf_ref[...])
        hb_ref[...] = hf.astype(BF16)
        gate_ref[...] = _router_gate(hf, wrg_ref[...], brg_ref[...], wre_ref[...], bre_ref[...],
                                     n_groups, epg)
        o_ref[...] = x

    hb = hb_ref[...]
    gate = gate_ref[...]
    lane_e = lax.broadcasted_iota(jnp.int32, gate.shape, 1)
    ge = jnp.sum(jnp.where(lane_e == e, gate, 0.0), axis=-1, keepdims=True)
    a = _dot(hb, wg_ref[0])
    b = _dot(hb, wu_ref[0])
    hid = (a * jax.nn.sigmoid(a)) * b
    o_ref[...] += _dot((hid * ge).astype(BF16), wd_ref[0])

    if final_norm:
        @pl.when(e == n_e - 1)
        def _():
            o_ref[...] = _rms(o_ref[...], fin_ref[...])


def _moe_dense(x, nf, wrg, brg, wre, bre, wg, wu, wd, fin, *, n_groups, epg, final_norm, tm):
    t, d = x.shape
    n_e, _, f = wg.shape
    tm = min(tm, t)
    kern = functools.partial(_moe_dense_kernel, n_groups=n_groups, epg=epg, final_norm=final_norm)
    return pl.pallas_call(
        kern,
        grid=(t // tm, n_e),
        in_specs=[pl.BlockSpec((tm, d), lambda i, e: (i, 0)),
                  _const_spec(nf.shape), _const_spec(wrg.shape), _const_spec(brg.shape),
                  _const_spec(wre.shape), _const_spec(bre.shape),
                  pl.BlockSpec((1, d, f), lambda i, e: (e, 0, 0)),
                  pl.BlockSpec((1, d, f), lambda i, e: (e, 0, 0)),
                  pl.BlockSpec((1, f, d), lambda i, e: (e, 0, 0)),
                  _const_spec(fin.shape)],
        out_specs=pl.BlockSpec((tm, d), lambda i, e: (i, 0)),
        out_shape=jax.ShapeDtypeStruct((t, d), F32),
        scratch_shapes=[pltpu.VMEM((tm, d), BF16), pltpu.VMEM((tm, n_e), F32)],
        compiler_params=_params(("arbitrary", "arbitrary")),
        name="moe_dense",
    )(x, nf, wrg, brg, wre, bre, wg, wu, wd, fin)


def _kvq_kernel(x_ref, nkv_ref, nq_ref, wk_ref, wv_ref, wq_ref, k_ref, v_ref, kb_ref, vb_ref, qb_ref,
                *, q_scale):
    x = x_ref[...]
    hk = _rms(x, nkv_ref[...]).astype(BF16)
    k = _dot(hk, wk_ref[...])
    v = _dot(hk, wv_ref[...])
    k_ref[...] = k
    v_ref[...] = v
    kb_ref[...] = k.astype(BF16)
    vb_ref[...] = v.astype(BF16)
    hq = _rms(x, nq_ref[...]).astype(BF16)
    qb_ref[...] = (_dot(hq, wq_ref[...]) * q_scale).astype(BF16)


def _kvq(x, nkv, nq, wk, wv, wq, *, q_scale, tm):
    t, d = x.shape
    tm = min(tm, t)
    row = pl.BlockSpec((tm, d), lambda i: (i, 0))
    return pl.pallas_call(
        functools.partial(_kvq_kernel, q_scale=q_scale),
        grid=(t // tm,),
        in_specs=[row, _const_spec(nkv.shape), _const_spec(nq.shape),
                  _const_spec(wk.shape), _const_spec(wv.shape), _const_spec(wq.shape)],
        out_specs=[row] * 5,
        out_shape=[jax.ShapeDtypeStruct((t, d), F32), jax.ShapeDtypeStruct((t, d), F32),
                   jax.ShapeDtypeStruct((t, d), BF16), jax.ShapeDtypeStruct((t, d), BF16),
                   jax.ShapeDtypeStruct((t, d), BF16)],
        compiler_params=_params(("arbitrary",)),
        name="kvq_proj",
    )(x, nkv, nq, wk, wv, wq)


def _lam_value(lq_ref, lk_ref, lam_init):
    lq = lq_ref[...]
    lk = lk_ref[...]
    s = jnp.sum(lq * lk, axis=-1, keepdims=True)
    return jnp.exp(s[0:1]) - jnp.exp(s[1:2]) + lam_init


def _attn_prompt_kernel(slopes_ref, q_ref, k_ref, v_ref, lq_ref, lk_ref, sub_ref, o_ref,
                        *, head_dim, lam_init):
    h = pl.program_id(1)
    qi = pl.program_id(2)
    tq, dv = q_ref.shape
    tk = tq
    slope = slopes_ref[h]
    q = q_ref[...].astype(F32)
    lane = lax.broadcasted_iota(jnp.int32, (tq, dv), 1)
    qq = jnp.concatenate([jnp.where(lane < head_dim, q, 0.0),
                          jnp.where(lane >= head_dim, q, 0.0)], axis=0).astype(BF16)
    ri = lax.broadcasted_iota(jnp.int32, (tq, tk), 0)
    ci = lax.broadcasted_iota(jnp.int32, (tq, tk), 1)
    rel = ri - ci
    ns_rel = (-slope) * rel.astype(F32)
    ns_rel2 = jnp.concatenate([ns_rel, ns_rel], axis=0)
    visible2 = jnp.concatenate([rel >= 0, rel >= 0], axis=0)

    def step(c, carry, diagonal):
        m, l, acc = carry
        k0 = pl.multiple_of(c * tk, tk)
        kc = k_ref[pl.ds(k0, tk), :]
        vc = v_ref[pl.ds(k0, tk), :]
        s = _dot_nt(qq, kc) + ns_rel2
        if diagonal:
            s = jnp.where(visible2, s, MASK_VALUE)
            cc = jnp.float32(0.0)
        else:
            cc = (-slope) * ((qi - c) * tq).astype(F32)
        m_new = jnp.maximum(m, jnp.max(s, axis=-1, keepdims=True) + cc)
        alpha = jnp.exp(m - m_new)
        p = jnp.exp(s + (cc - m_new))
        l = alpha * l + jnp.sum(p, axis=-1, keepdims=True)
        acc = alpha * acc + _dot(p.astype(BF16), vc)
        return m_new, l, acc

    init = (jnp.full((2 * tq, 1), -jnp.inf, F32), jnp.zeros((2 * tq, 1), F32),
            jnp.zeros((2 * tq, dv), F32))
    carry = lax.fori_loop(0, qi, lambda c, cr: step(c, cr, False), init)
    m, l, acc = step(qi, carry, True)
    o2 = acc / l
    lam = _lam_value(lq_ref, lk_ref, lam_init)
    o = o2[:tq] - lam * o2[tq:]
    o_ref[...] = (_rms(o, sub_ref[...]) * (1.0 - lam_init)).astype(o_ref.dtype)


def _attn_prompt(slopes, qb, kb, vb, lq, lk, sub, *, bsz, seq, n_heads, head_dim, lam_init, tq):
    t, d = qb.shape
    dv = d // n_heads
    tq = min(tq, seq)
    nq = seq // tq
    kern = functools.partial(_attn_prompt_kernel, head_dim=head_dim, lam_init=lam_init)
    return pl.pallas_call(
        kern,
        grid_spec=pltpu.PrefetchScalarGridSpec(
            num_scalar_prefetch=0,
            grid=(bsz, n_heads, nq),
            in_specs=[pl.BlockSpec(memory_space=pltpu.SMEM),
                      pl.BlockSpec((tq, dv), lambda b, h, i: (b * nq + i, h)),
                      pl.BlockSpec((seq, dv), lambda b, h, i: (b, h)),
                      pl.BlockSpec((seq, dv), lambda b, h, i: (b, h)),
                      _const_spec(lq.shape), _const_spec(lk.shape), _const_spec(sub.shape)],
            out_specs=pl.BlockSpec((tq, dv), lambda b, h, i: (b * nq + i, h)),
        ),
        out_shape=jax.ShapeDtypeStruct((t, d), BF16),
        compiler_params=_params(("arbitrary", "arbitrary", "arbitrary")),
        name="attn_prompt",
    )(slopes, qb, kb, vb, lq, lk, sub)


def _attn_sample_kernel(pt_ref, q_ref, ks_ref, vs_ref, slope_ref, lq_ref, lk_ref, sub_ref, *refs,
                        pps, n_heads, head_dim, past_len, lam_init):
    k_refs = refs[:pps]
    v_refs = refs[pps:2 * pps]
    o_ref, m_ref, l_ref, acc_ref = refs[2 * pps:]
    j = pl.program_id(1)
    nj = pl.num_programs(1)
    n_rows = 2 * n_heads
    page = k_refs[0].shape[1]
    d = q_ref.shape[2]
    dv = d // n_heads

    @pl.when(j == 0)
    def _():
        m_ref[...] = jnp.full(m_ref.shape, -jnp.inf, F32)
        l_ref[...] = jnp.zeros(l_ref.shape, F32)
        acc_ref[...] = jnp.zeros(acc_ref.shape, F32)

    q = q_ref[0]
    r_id = lax.broadcasted_iota(jnp.int32, (n_rows, d), 0)
    l_id = lax.broadcasted_iota(jnp.int32, (n_rows, d), 1)
    r_head = jnp.where(r_id >= n_heads, r_id - n_heads, r_id)
    r_map = jnp.where(r_id >= n_heads, 1, 0)
    sel = (l_id >= r_head * dv + r_map * head_dim) & (l_id < r_head * dv + r_map * head_dim + head_dim)
    qm = jnp.where(sel, jnp.broadcast_to(q.astype(F32), (n_rows, d)), 0.0).astype(BF16)
    slope = slope_ref[...]
    pos_l = lax.broadcasted_iota(jnp.int32, (n_rows, page), 1)

    m = m_ref[...]
    l = l_ref[...]
    acc = acc_ref[...]
    for i in range(pps):
        kp = k_refs[i][0].astype(BF16)
        vp = v_refs[i][0].astype(BF16)
        s = _dot_nt(qm, kp)
        dist = (past_len - (j * pps + i) * page) - pos_l
        s = s - slope * dist.astype(F32)
        m_new = jnp.maximum(m, jnp.max(s, axis=-1, keepdims=True))
        alpha = jnp.exp(m - m_new)
        p = jnp.exp(s - m_new)
        l = alpha * l + jnp.sum(p, axis=-1, keepdims=True)
        acc = alpha * acc + _dot(p.astype(BF16), vp)
        m = m_new
    m_ref[...] = m
    l_ref[...] = l
    acc_ref[...] = acc

    @pl.when(j == nj - 1)
    def _():
        ks = ks_ref[0].astype(BF16).astype(F32)
        vs = vs_ref[0].astype(BF16).astype(F32)
        s_self = jnp.sum(qm.astype(F32) * ks, axis=-1, keepdims=True)
        m_new = jnp.maximum(m, s_self)
        alpha = jnp.exp(m - m_new)
        p = jnp.exp(s_self - m_new)
        lf = alpha * l + p
        af = alpha * acc + p.astype(BF16).astype(F32) * vs
        o2 = af / lf
        lam = _lam_value(lq_ref, lk_ref, lam_init)
        o = o2[:n_heads] - lam * o2[n_heads:]
        rh = lax.broadcasted_iota(jnp.int32, (n_heads, d), 0)
        lh = lax.broadcasted_iota(jnp.int32, (n_heads, d), 1)
        own = (lh >= rh * dv) & (lh < rh * dv + dv)
        od = jnp.where(own, o, 0.0)
        ms = jnp.sum(od * od, axis=-1, keepdims=True) * (1.0 / dv)
        on = od * lax.rsqrt(ms + RMS_EPS)
        row = jnp.sum(on, axis=0, keepdims=True)
        o_ref[0] = (row * sub_ref[...] * (1.0 - lam_init)).astype(o_ref.dtype)


def _attn_sample(page_table, qb, ks, vs, slope_rows, lq, lk, sub_t, cache_k, cache_v, *,
                 n_heads, head_dim, lam_init, pps):
    bd, n_pages = page_table.shape
    n_pool, page, d = cache_k.shape
    pps = math.gcd(pps, n_pages)
    past_len = n_pages * page
    n_rows = 2 * n_heads
    pt = page_table.reshape(-1)
    q3 = qb.reshape(bd, 1, d)
    ks3 = ks.reshape(bd, 1, d)
    vs3 = vs.reshape(bd, 1, d)
    tok = pl.BlockSpec((1, 1, d), lambda b, j, pt: (b, 0, 0))

    def page_spec(i):
        return pl.BlockSpec((1, page, d), lambda b, j, pt: (pt[b * n_pages + j * pps + i], 0, 0))

    const2 = lambda shape: pl.BlockSpec(shape, lambda b, j, pt: (0, 0))
    kern = functools.partial(_attn_sample_kernel, pps=pps, n_heads=n_heads, head_dim=head_dim,
                             past_len=past_len, lam_init=lam_init)
    out = pl.pallas_call(
        kern,
        grid_spec=pltpu.PrefetchScalarGridSpec(
            num_scalar_prefetch=1,
            grid=(bd, n_pages // pps),
            in_specs=[tok, tok, tok, const2(slope_rows.shape), const2(lq.shape), const2(lk.shape),
                      const2(sub_t.shape)]
                     + [page_spec(i) for i in range(pps)] + [page_spec(i) for i in range(pps)],
            out_specs=tok,
            scratch_shapes=[pltpu.VMEM((n_rows, 1), F32), pltpu.VMEM((n_rows, 1), F32),
                            pltpu.VMEM((n_rows, d), F32)],
        ),
        out_shape=jax.ShapeDtypeStruct((bd, 1, d), BF16),
        compiler_params=_params(("arbitrary", "arbitrary")),
        name="attn_sample",
    )(pt, q3, ks3, vs3, slope_rows, lq, lk, sub_t, *([cache_k] * pps), *([cache_v] * pps))
    return out.reshape(bd, d)


def _oproj_kernel(x_ref, a_ref, w_ref, o_ref):
    o_ref[...] = x_ref[...] + _dot(a_ref[...], w_ref[...])


def _oproj(x, a, w, *, tm):
    t, d = x.shape
    tm = min(tm, t)
    row = pl.BlockSpec((tm, d), lambda i: (i, 0))
    return pl.pallas_call(
        _oproj_kernel,
        grid=(t // tm,),
        in_specs=[row, pl.BlockSpec((tm, a.shape[1]), lambda i: (i, 0)), _const_spec(w.shape)],
        out_specs=row,
        out_shape=jax.ShapeDtypeStruct((t, d), F32),
        compiler_params=_params(("arbitrary",)),
        name="attn_oproj",
    )(x, a, w)


def kernel(x_prompt, x_sample, cache_k, cache_v, page_table, norm_mix, norm_ffn, w_uv, sgu_gain, w_sgu, b_sgu, w_gmlp_o, norm_kv, w_k, w_v, w_q, lam_q, lam_k, subln_gain, w_attn_o, w_router_group, b_router_group, w_router_expert, b_router_expert, w_e_gate, w_e_up, w_e_down, norm_final):
    bsz, seq, d = x_prompt.shape
    dec_b, dec_s, _ = x_sample.shape
    assert dec_s == 1, "sample group handles one new token per sequence"
    n_pool, page, n_heads, _, head_dim = cache_k.shape
    v_dim = cache_v.shape[-1]
    assert n_heads * v_dim == d and 2 * head_dim == v_dim
    n_a = w_uv.shape[0]
    depth = norm_mix.shape[0]
    assert n_a == 1 and depth == 2, "one gMLP layer followed by one attention layer"
    _, n_sgu, chunk, _ = w_sgu.shape
    d_gm = sgu_gain.shape[1]
    sgw = d_gm // n_sgu
    n_groups = w_router_group.shape[2]
    epg = w_router_expert.shape[3]
    n_e = n_groups * epg
    assert seq % chunk == 0

    row2 = lambda a: a.reshape(1, -1)
    wuv = w_uv[0].astype(BF16)
    wo_g = w_gmlp_o[0].astype(BF16)
    wk = w_k.astype(BF16)
    wv = w_v.astype(BF16)
    wq = w_q[0].astype(BF16)
    wao = w_attn_o[0].astype(BF16)
    wg = w_e_gate.astype(BF16)
    wu = w_e_up.astype(BF16)
    wd = w_e_down.astype(BF16)
    wre = jnp.transpose(w_router_expert, (0, 2, 1, 3)).reshape(depth, d, n_e)
    bre = b_router_expert.reshape(depth, 1, n_e)
    brg = b_router_group.reshape(depth, 1, n_groups)
    fin = row2(norm_final)
    slopes = 2.0 ** (-8.0 * jnp.arange(1, n_heads + 1, dtype=F32) / n_heads)
    lam_init = 0.8 - 0.6 * math.exp(-0.3 * 1)
    q_scale = head_dim ** -0.5
    ws_row = jnp.repeat(w_sgu[0, :, 0, 0], sgw).reshape(1, d_gm)
    bs_row = jnp.repeat(b_sgu[0, :, 0], sgw).reshape(1, d_gm)
    bs_t = b_sgu[0].T
    sub_t = jnp.tile(subln_gain[0], n_heads).reshape(1, d)
    slope_rows = jnp.tile(slopes, 2).reshape(2 * n_heads, 1)

    def moe(x, i, final_norm, tm):
        return _moe_dense(x, row2(norm_ffn[i]), w_router_group[i], brg[i], wre[i], bre[i],
                          wg[i], wu[i], wd[i], fin, n_groups=n_groups, epg=epg,
                          final_norm=final_norm, tm=tm)

    xp = x_prompt.reshape(bsz * seq, d)
    (xp,) = _gmlp(xp, row2(norm_mix[0]), wuv, row2(sgu_gain[0]), w_sgu[0], bs_t, wo_g,
                  seq_one=False, chunk=chunk, n_groups=n_sgu, tm=256)
    xp = moe(xp, 0, False, 1024)
    k_p, v_p, kb, vb, qb = _kvq(xp, row2(norm_kv), row2(norm_mix[1]), wk, wv, wq,
                                q_scale=q_scale, tm=512)
    att = _attn_prompt(slopes, qb, kb, vb, lam_q[0], lam_k[0], row2(subln_gain[0]),
                       bsz=bsz, seq=seq, n_heads=n_heads, head_dim=head_dim,
                       lam_init=lam_init, tq=256)
    xp = _oproj(xp, att, wao, tm=512)
    y_prompt = moe(xp, 1, True, 1024).reshape(bsz, seq, d)

    xs = x_sample.reshape(dec_b, d)
    xs, gv = _gmlp(xs, row2(norm_mix[0]), wuv, row2(sgu_gain[0]), ws_row, bs_row, wo_g,
                   seq_one=True, chunk=chunk, n_groups=n_sgu, tm=128)
    xs = moe(xs, 0, False, 128)
    k_s, v_s, _, _, qs = _kvq(xs, row2(norm_kv), row2(norm_mix[1]), wk, wv, wq,
                              q_scale=q_scale, tm=128)
    att_s = _attn_sample(page_table, qs, k_s, v_s, slope_rows, lam_q[0], lam_k[0], sub_t,
                         cache_k.reshape(n_pool, page, d), cache_v.reshape(n_pool, page, d),
                         n_heads=n_heads, head_dim=head_dim, lam_init=lam_init, pps=4)
    xs = _oproj(xs, att_s, wao, tm=128)
    y_sample = moe(xs, 1, True, 128).reshape(dec_b, 1, d)

    n_pp = seq // page
    k_prompt = k_p.reshape(bsz, n_pp, page, n_heads, 2, head_dim)
    v_prompt = v_p.reshape(bsz, n_pp, page, n_heads, v_dim)
    k_sample = k_s.reshape(dec_b, 1, n_heads, 2, head_dim)
    v_sample = v_s.reshape(dec_b, 1, n_heads, v_dim)
    gmlp_v_sample = gv.reshape(n_a, dec_b, 1, d_gm)
    return (y_prompt, y_sample, k_prompt, v_prompt, k_sample, v_sample, gmlp_v_sample)
```

```python
import functools
import math

import jax
import jax.numpy as jnp
from jax import lax
from jax.experimental import pallas as pl
from jax.experimental.pallas import tpu as pltpu

F32 = jnp.float32
BF16 = jnp.bfloat16

RMS_EPS = 1e-6
MASK_VALUE = -1e30
TOP_K_INNER = 2
LANES = 128
SUBLANES = 8
VMEM_LIMIT_BYTES = 56 * 1024 * 1024
POS_SPLIT = 256


def _rms(x, g):
    ms = jnp.mean(x * x, axis=-1, keepdims=True)
    return x * lax.rsqrt(ms + RMS_EPS) * g


def _gelu_tanh(x):
    c = math.sqrt(2.0 / math.pi)
    return x * (0.5 * (1.0 + jnp.tanh(c * (x + 0.044715 * (x * x * x)))))


def _dot(a, b):
    return jnp.dot(a, b, preferred_element_type=F32)


def _params(sem):
    return pltpu.CompilerParams(dimension_semantics=sem, vmem_limit_bytes=VMEM_LIMIT_BYTES)


def _const_spec(shape):
    nd = len(shape)
    return pl.BlockSpec(shape, lambda *_: (0,) * nd)


def _gmlp_kernel(x_ref, nm_ref, wuv_ref, sg_ref, ws_ref, bs_ref, wo_ref, *out_refs,
                 seq_one, chunk, n_groups, slab_groups):
    if seq_one:
        o_ref, v_ref = out_refs
    else:
        (o_ref,) = out_refs
    tm, _ = x_ref.shape
    d_gm = sg_ref.shape[1]
    sgw = d_gm // n_groups
    x = x_ref[...]
    h = _rms(x, nm_ref[...]).astype(BF16)
    v = _gelu_tanh(_dot(h, wuv_ref[:, d_gm:]))
    vn = _rms(v, sg_ref[...])
    if seq_one:
        v_ref[...] = vn
        vb = vn.astype(BF16).astype(F32)
    else:
        vb = vn.astype(BF16)
        row = lax.broadcasted_iota(jnp.int32, (chunk, chunk), 0)
        col = lax.broadcasted_iota(jnp.int32, (chunk, chunk), 1)
        causal = row >= col
    slab = slab_groups * sgw
    acc = x
    for s in range(n_groups // slab_groups):
        c0 = s * slab
        u = _gelu_tanh(_dot(h, wuv_ref[:, c0:c0 + slab]))
        if seq_one:
            gate = vb[:, c0:c0 + slab] * ws_ref[:, c0:c0 + slab] + bs_ref[:, c0:c0 + slab]
        else:
            cols = []
            for gg in range(slab_groups):
                g = s * slab_groups + gg
                w = jnp.where(causal, ws_ref[g], 0.0).astype(BF16)
                bias = jnp.broadcast_to(bs_ref[:, g:g + 1], (chunk, sgw))
                rows = []
                for c in range(tm // chunk):
                    vv = vb[c * chunk:(c + 1) * chunk, g * sgw:(g + 1) * sgw]
                    rows.append(_dot(w, vv) + bias)
                cols.append(jnp.concatenate(rows, axis=0) if len(rows) > 1 else rows[0])
            gate = jnp.concatenate(cols, axis=1) if len(cols) > 1 else cols[0]
        p = (u * gate).astype(BF16)
        acc = acc + _dot(p, wo_ref[c0:c0 + slab, :])
    o_ref[...] = acc


def _gmlp(x, nm, wuv, sg, ws, bs, wo, *, seq_one, chunk, n_groups, tm):
    t, d = x.shape
    d_gm = sg.shape[1]
    tm = min(tm, t)
    slab_groups = 2 if n_groups % 2 == 0 else 1
    out_shape = [jax.ShapeDtypeStruct((t, d), F32)]
    out_specs = [pl.BlockSpec((tm, d), lambda i: (i, 0))]
    if seq_one:
        out_shape.append(jax.ShapeDtypeStruct((t, d_gm), F32))
        out_specs.append(pl.BlockSpec((tm, d_gm), lambda i: (i, 0)))
    kern = functools.partial(_gmlp_kernel, seq_one=seq_one, chunk=chunk, n_groups=n_groups,
                             slab_groups=slab_groups)
    return pl.pallas_call(
        kern,
        grid=(t // tm,),
        in_specs=[pl.BlockSpec((tm, d), lambda i: (i, 0)),
                  _const_spec(nm.shape), _const_spec(wuv.shape), _const_spec(sg.shape),
                  _const_spec(ws.shape), _const_spec(bs.shape), _const_spec(wo.shape)],
        out_specs=out_specs,
        out_shape=out_shape,
        compiler_params=_params(("arbitrary",)),
        name="gmlp_seq1" if seq_one else "gmlp",
    )(x, nm, wuv, sg, ws, bs, wo)


def _router_gate(hf, wrg, brg, wre, bre, n_groups, epg):
    tm = hf.shape[0]
    n_e = n_groups * epg
    lg = jnp.dot(hf, wrg, preferred_element_type=F32, precision=lax.Precision.HIGHEST) + brg
    lane_g = lax.broadcasted_iota(jnp.int32, (tm, n_groups), 1)
    mg = jnp.max(lg, axis=-1, keepdims=True)
    p_grp = 1.0 / jnp.sum(jnp.exp(lg - mg), axis=-1, keepdims=True)
    grp = jnp.min(jnp.where(lg == mg, lane_g, n_groups), axis=-1, keepdims=True)
    el = jnp.dot(hf, wre, preferred_element_type=F32, precision=lax.Precision.HIGHEST) + bre
    lane_e = lax.broadcasted_iota(jnp.int32, (tm, n_e), 1)
    neg = jnp.float32(-jnp.inf)
    in_grp = (lane_e >= grp * epg) & (lane_e < grp * epg + epg)
    el1 = jnp.where(in_grp, el, neg)
    v1 = jnp.max(el1, axis=-1, keepdims=True)
    i1 = jnp.min(jnp.where(el1 == v1, lane_e, n_e), axis=-1, keepdims=True)
    el2 = jnp.where(lane_e == i1, neg, el1)
    v2 = jnp.max(el2, axis=-1, keepdims=True)
    i2 = jnp.min(jnp.where(el2 == v2, lane_e, n_e), axis=-1, keepdims=True)
    e2 = jnp.exp(v2 - v1)
    w1 = 1.0 / (1.0 + e2)
    w2 = e2 / (1.0 + e2)
    gate = jnp.where(lane_e == i1, w1 * p_grp, 0.0) + jnp.where(lane_e == i2, w2 * p_grp, 0.0)
    return gate


def _moe_dense_kernel(x_ref, nf_ref, wrg_ref, brg_ref, wre_ref, bre_ref, wg_ref, wu_ref, wd_ref,
                      fin_ref, o_ref, hb_ref, gate_ref, *, n_groups, epg, final_norm):
    e = pl.program_id(1)
    n_e = n_groups * epg

    @pl.when(e == 0)
    def _():
        x = x_ref[...]
        hf = _rms(x, nf_ref[...])
        hb_ref[...] = hf.astype(BF16)
        gate_ref[...] = _router_gate(hf, wrg_ref[...], brg_ref[...], wre_ref[...], bre_ref[...],
                                     n_groups, epg)
        o_ref[...] = x

    hb = hb_ref[...]
    gate = gate_ref[...]
    lane_e = lax.broadcasted_iota(jnp.int32, gate.shape, 1)
    ge = jnp.sum(jnp.where(lane_e == e, gate, 0.0), axis=-1, keepdims=True)
    a = _dot(hb, wg_ref[0])
    b = _dot(hb, wu_ref[0])
    hid = (a * jax.nn.sigmoid(a)) * b
    o_ref[...] += _dot((hid * ge).astype(BF16), wd_ref[0])

    if final_norm:
        @pl.when(e == n_e - 1)
        def _():
            o_ref[...] = _rms(o_ref[...], fin_ref[...])


def _moe_dense(x, nf, wrg, brg, wre, bre, wg, wu, wd, fin, *, n_groups, epg, final_norm, tm):
    t, d = x.shape
    n_e, _, f = wg.shape
    tm = min(tm, t)
    kern = functools.partial(_moe_dense_kernel, n_groups=n_groups, epg=epg, final_norm=final_norm)
    return pl.pallas_call(
        kern,
        grid=(t // tm, n_e),
        in_specs=[pl.BlockSpec((tm, d), lambda i, e: (i, 0)),
                  _const_spec(nf.shape), _const_spec(wrg.shape), _const_spec(brg.shape),
                  _const_spec(wre.shape), _const_spec(bre.shape),
                  pl.BlockSpec((1, d, f), lambda i, e: (e, 0, 0)),
                  pl.BlockSpec((1, d, f), lambda i, e: (e, 0, 0)),
                  pl.BlockSpec((1, f, d), lambda i, e: (e, 0, 0)),
                  _const_spec(fin.shape)],
        out_specs=pl.BlockSpec((tm, d), lambda i, e: (i, 0)),
        out_shape=jax.ShapeDtypeStruct((t, d), F32),
        scratch_shapes=[pltpu.VMEM((tm, d), BF16), pltpu.VMEM((tm, n_e), F32)],
        compiler_params=_params(("arbitrary", "arbitrary")),
        name="moe_dense",
    )(x, nf, wrg, brg, wre, bre, wg, wu, wd, fin)


def _kvq_prompt_kernel(x_ref, nkv_ref, nq_ref, wk_ref, wv_ref, wq_ref,
                       kt_ref, v3_ref, kb_ref, vbt_ref, qbt_ref, *, q_scale, page, tk):
    tm, d = x_ref.shape
    n_heads, dv = v3_ref.shape[1:]
    x = x_ref[...]
    hk = _rms(x, nkv_ref[...]).astype(BF16)
    k = _dot(hk, wk_ref[...])
    v = _dot(hk, wv_ref[...])
    kb_ref[...] = k.astype(BF16)
    for pg in range(tm // page):
        kt_ref[pg * d:(pg + 1) * d, :] = k[pg * page:(pg + 1) * page, :].T
    for h in range(n_heads):
        v3_ref[:, h, :] = v[:, h * dv:(h + 1) * dv]
    hq = _rms(x, nq_ref[...]).astype(BF16)
    q = _dot(hq, wq_ref[...]) * q_scale
    for c in range(tm // tk):
        vbt_ref[0, c] = v[c * tk:(c + 1) * tk, :].T.astype(BF16)
        qbt_ref[0, c] = q[c * tk:(c + 1) * tk, :].T.astype(BF16)


def _kvq_prompt(x, nkv, nq, wk, wv, wq, *, bsz, seq, n_heads, q_scale, page, tk, tm):
    t, d = x.shape
    dv = d // n_heads
    tm = min(tm, seq)
    nt = seq // tm
    row = pl.BlockSpec((tm, d), lambda i: (i, 0))
    tspec = pl.BlockSpec((1, tm // tk, d, tk), lambda i: (i // nt, i % nt, 0, 0))
    kern = functools.partial(_kvq_prompt_kernel, q_scale=q_scale, page=page, tk=tk)
    return pl.pallas_call(
        kern,
        grid=(t // tm,),
        in_specs=[row, _const_spec(nkv.shape), _const_spec(nq.shape),
                  _const_spec(wk.shape), _const_spec(wv.shape), _const_spec(wq.shape)],
        out_specs=[pl.BlockSpec((tm // page * d, page), lambda i: (i, 0)),
                   pl.BlockSpec((tm, n_heads, dv), lambda i: (i, 0, 0)),
                   row, tspec, tspec],
        out_shape=[jax.ShapeDtypeStruct((t // page * d, page), F32),
                   jax.ShapeDtypeStruct((t, n_heads, dv), F32),
                   jax.ShapeDtypeStruct((t, d), BF16),
                   jax.ShapeDtypeStruct((bsz, seq // tk, d, tk), BF16),
                   jax.ShapeDtypeStruct((bsz, seq // tk, d, tk), BF16)],
        compiler_params=_params(("arbitrary",)),
        name="kvq_prompt",
    )(x, nkv, nq, wk, wv, wq)


def _kvq_sample_kernel(x_ref, nkv_ref, nq_ref, wk_ref, wv_ref, wq_ref,
                       kt_ref, k_ref, v3_ref, q_ref, *, q_scale):
    n_heads, dv = v3_ref.shape[1:]
    x = x_ref[...]
    hk = _rms(x, nkv_ref[...]).astype(BF16)
    k = _dot(hk, wk_ref[...])
    v = _dot(hk, wv_ref[...])
    k_ref[...] = k
    kt_ref[...] = k.T
    for h in range(n_heads):
        v3_ref[:, h, :] = v[:, h * dv:(h + 1) * dv]
    hq = _rms(x, nq_ref[...]).astype(BF16)
    q_ref[...] = _dot(hq, wq_ref[...]) * q_scale


def _kvq_sample(x, nkv, nq, wk, wv, wq, *, n_heads, q_scale):
    t, d = x.shape
    dv = d // n_heads
    kern = functools.partial(_kvq_sample_kernel, q_scale=q_scale)
    return pl.pallas_call(
        kern,
        grid=(1,),
        in_specs=[_const_spec(x.shape), _const_spec(nkv.shape), _const_spec(nq.shape),
                  _const_spec(wk.shape), _const_spec(wv.shape), _const_spec(wq.shape)],
        out_specs=[_const_spec((d, t)), _const_spec((t, d)), _const_spec((t, n_heads, dv)),
                   _const_spec((t, d))],
        out_shape=[jax.ShapeDtypeStruct((d, t), F32), jax.ShapeDtypeStruct((t, d), F32),
                   jax.ShapeDtypeStruct((t, n_heads, dv), F32), jax.ShapeDtypeStruct((t, d), F32)],
        compiler_params=_params(("arbitrary",)),
        name="kvq_sample",
    )(x, nkv, nq, wk, wv, wq)


def _lam_value(lq_ref, lk_ref, lam_init):
    lq = lq_ref[...]
    lk = lk_ref[...]
    s = jnp.sum(lq * lk, axis=-1, keepdims=True)
    return jnp.exp(s[0:1]) - jnp.exp(s[1:2]) + lam_init


def _attn_prompt_kernel(slopes_ref, qt_ref, k_ref, kpos_ref, vt_ref, lq_ref, lk_ref, sub_ref, o_ref,
                        *, head_dim, lam_init):
    h = pl.program_id(1)
    qi = pl.program_id(2)
    dv, tq = qt_ref.shape[2:]
    tk = tq
    nf = kpos_ref.shape[1]
    slope = slopes_ref[h]
    qt = qt_ref[0, 0].astype(F32)
    row = lax.broadcasted_iota(jnp.int32, (dv, tq), 0)
    qq = jnp.concatenate([jnp.where(row < head_dim, qt, 0.0),
                          jnp.where(row >= head_dim, qt, 0.0)], axis=1)
    fr = lax.broadcasted_iota(jnp.int32, (nf, 2 * tq), 0)
    ci = lax.broadcasted_iota(jnp.int32, (nf, 2 * tq), 1)
    ipos = qi * tq + jnp.where(ci >= tq, ci - tq, ci)
    ihi = ((ipos // POS_SPLIT) * POS_SPLIT).astype(F32)
    ilo = (ipos % POS_SPLIT).astype(F32)
    feat = jnp.where(fr == 0, -slope * ihi,
                     jnp.where(fr == 1, -slope * ilo,
                               jnp.where((fr == 2) | (fr == 3), slope, 0.0)))
    qqa = jnp.concatenate([qq, feat], axis=0).astype(BF16)
    kr = lax.broadcasted_iota(jnp.int32, (tk, 2 * tq), 0)
    kc_i = lax.broadcasted_iota(jnp.int32, (tk, 2 * tq), 1)
    visible = jnp.where(kc_i >= tq, kc_i - tq, kc_i) >= kr

    def step(c, carry, diagonal):
        m, l, acc = carry
        k0 = pl.multiple_of(c * tk, tk)
        ka = jnp.concatenate([k_ref[pl.ds(k0, tk), :], kpos_ref[pl.ds(k0, tk), :]], axis=1)
        s = _dot(ka, qqa)
        if diagonal:
            s = jnp.where(visible, s, MASK_VALUE)
        m_new = jnp.maximum(m, jnp.max(s, axis=0, keepdims=True))
        alpha = jnp.exp(m - m_new)
        p = jnp.exp(s - m_new)
        l = alpha * l + jnp.sum(p, axis=0, keepdims=True)
        acc = alpha * acc + _dot(vt_ref[0, c], p.astype(BF16))
        return m_new, l, acc

    init = (jnp.full((1, 2 * tq), -jnp.inf, F32), jnp.zeros((1, 2 * tq), F32),
            jnp.zeros((dv, 2 * tq), F32))
    carry = lax.fori_loop(0, qi, lambda c, cr: step(c, cr, False), init)
    m, l, acc = step(qi, carry, True)
    o2 = acc / l
    lam = _lam_value(lq_ref, lk_ref, lam_init)
    ot = o2[:, :tq] - lam * o2[:, tq:]
    ms = jnp.mean(ot * ot, axis=0, keepdims=True)
    ot = ot * lax.rsqrt(ms + RMS_EPS)
    o_ref[...] = (ot.T * sub_ref[...] * (1.0 - lam_init)).astype(o_ref.dtype)


def _attn_prompt(slopes, qbt, kb, kpos, vbt, lq, lk, sub, *, bsz, seq, n_heads, head_dim, lam_init):
    _, nq, d, tq = qbt.shape
    dv = d // n_heads
    kern = functools.partial(_attn_prompt_kernel, head_dim=head_dim, lam_init=lam_init)
    return pl.pallas_call(
        kern,
        grid=(bsz, n_heads, nq),
        in_specs=[pl.BlockSpec(memory_space=pltpu.SMEM),
                  pl.BlockSpec((1, 1, dv, tq), lambda b, h, i: (b, i, h, 0)),
                  pl.BlockSpec((seq, dv), lambda b, h, i: (b, h)),
                  pl.BlockSpec(kpos.shape, lambda b, h, i: (0, 0)),
                  pl.BlockSpec((1, nq, dv, tq), lambda b, h, i: (b, 0, h, 0)),
                  pl.BlockSpec(lq.shape, lambda b, h, i: (0, 0)),
                  pl.BlockSpec(lk.shape, lambda b, h, i: (0, 0)),
                  pl.BlockSpec(sub.shape, lambda b, h, i: (0, 0))],
        out_specs=pl.BlockSpec((tq, dv), lambda b, h, i: (b * nq + i, h)),
        out_shape=jax.ShapeDtypeStruct((bsz * seq, d), BF16),
        compiler_params=_params(("arbitrary", "arbitrary", "arbitrary")),
        name="attn_prompt",
    )(slopes, qbt, kb, kpos, vbt, lq, lk, sub)


def _lane_spread(row, n):
    return jnp.broadcast_to(row, (n, row.shape[1])).T


def _sample_scores_kernel(pt_ref, q_ref, ks_ref, slope_ref, *refs, pps, n_maps, past_len):
    k_refs = refs[:pps]
    s_ref, sself_ref, qb_ref = refs[pps:]
    j = pl.program_id(1)
    d, page = qb_ref.shape
    n_heads = n_maps // 2
    hd = d // n_maps

    def map_sums(t):
        t3 = t.reshape(n_heads, 2 * hd, page)
        return jnp.concatenate([jnp.sum(t3[:, :hd, :], axis=1), jnp.sum(t3[:, hd:, :], axis=1)],
                               axis=0)

    @pl.when(j == 0)
    def _():
        qb = _lane_spread(q_ref[0], page)
        qb_ref[...] = qb
        sself_ref[0] = map_sums(qb * _lane_spread(ks_ref[0], page))

    qb = qb_ref[...]
    slope = slope_ref[...]
    lane = lax.broadcasted_iota(jnp.int32, (n_maps, page), 1)
    for i in range(pps):
        s = map_sums(k_refs[i][0] * qb)
        dist = (past_len - (j * pps + i) * page) - lane
        s_ref[0, :, i * page:(i + 1) * page] = s - slope * dist.astype(F32)


def _sample_scores(pt, q3, ks3, slope_rows, cache_kt, *, n_pages, pps):
    bd, _, d = q3.shape
    n_pool, _, page = cache_kt.shape
    n_maps = slope_rows.shape[0]
    past_len = n_pages * page
    tok = pl.BlockSpec((1, 1, d), lambda b, j, pt: (b, 0, 0))

    def page_spec(i):
        return pl.BlockSpec((1, d, page), lambda b, j, pt: (pt[b * n_pages + j * pps + i], 0, 0))

    kern = functools.partial(_sample_scores_kernel, pps=pps, n_maps=n_maps, past_len=past_len)
    return pl.pallas_call(
        kern,
        grid_spec=pltpu.PrefetchScalarGridSpec(
            num_scalar_prefetch=1,
            grid=(bd, n_pages // pps),
            in_specs=[tok, tok, pl.BlockSpec(slope_rows.shape, lambda b, j, pt: (0, 0))]
                     + [page_spec(i) for i in range(pps)],
            out_specs=[pl.BlockSpec((1, n_maps, pps * page), lambda b, j, pt: (b, 0, j)),
                       pl.BlockSpec((1, n_maps, page), lambda b, j, pt: (b, 0, 0))],
            scratch_shapes=[pltpu.VMEM((d, page), F32)],
        ),
        out_shape=[jax.ShapeDtypeStruct((bd, n_maps, past_len), F32),
                   jax.ShapeDtypeStruct((bd, n_maps, page), F32)],
        compiler_params=_params(("arbitrary", "arbitrary")),
        name="sample_scores",
    )(pt, q3, ks3, slope_rows, *([cache_kt] * pps))


def _sample_probs_kernel(s_ref, sself_ref, lq_ref, lk_ref, a_ref, aself_ref, *, n_heads, lam_init):
    lam = _lam_value(lq_ref, lk_ref, lam_init)

    def normalized(mp):
        s = s_ref[:, mp * n_heads:(mp + 1) * n_heads, :]
        ss = sself_ref[:, mp * n_heads:(mp + 1) * n_heads, :]
        m = jnp.maximum(jnp.max(s, axis=-1, keepdims=True), ss[:, :, :1])
        p = jnp.exp(s - m)
        ps = jnp.exp(ss - m)
        l = jnp.sum(p, axis=-1, keepdims=True) + ps[:, :, :1]
        return p / l, ps / l

    p0, ps0 = normalized(0)
    p1, ps1 = normalized(1)
    a_ref[...] = p0 - lam * p1
    aself_ref[...] = ps0 - lam * ps1


def _sample_probs(s, sself, lq, lk, *, n_heads, lam_init, sb):
    bd, n_maps, past = s.shape
    lanes = sself.shape[2]
    sb = math.gcd(sb, bd)
    kern = functools.partial(_sample_probs_kernel, n_heads=n_heads, lam_init=lam_init)
    return pl.pallas_call(
        kern,
        grid=(bd // sb,),
        in_specs=[pl.BlockSpec((sb, n_maps, past), lambda i: (i, 0, 0)),
                  pl.BlockSpec((sb, n_maps, lanes), lambda i: (i, 0, 0)),
                  _const_spec(lq.shape), _const_spec(lk.shape)],
        out_specs=[pl.BlockSpec((sb, n_heads, past), lambda i: (i, 0, 0)),
                   pl.BlockSpec((sb, n_heads, lanes), lambda i: (i, 0, 0))],
        out_shape=[jax.ShapeDtypeStruct((bd, n_heads, past), F32),
                   jax.ShapeDtypeStruct((bd, n_heads, lanes), F32)],
        compiler_params=_params(("arbitrary",)),
        name="sample_probs",
    )(s, sself, lq, lk)


def _sample_values_kernel(pt_ref, at_ref, aself_ref, vs_ref, sub_ref, *refs, pps, lam_init):
    v_refs = refs[:pps]
    o_ref, acc_ref = refs[pps:]
    j = pl.program_id(1)
    nj = pl.num_programs(1)
    n_heads, dv = acc_ref.shape
    rows, lanes = v_refs[0].shape[1:]
    pos_per_blk = lanes // n_heads
    blks = rows // lanes

    @pl.when(j == 0)
    def _():
        acc_ref[...] = jnp.zeros(acc_ref.shape, F32)

    acc = acc_ref[...]
    for i in range(pps):
        for blk in range(blks):
            r = at_ref[0, pl.ds((j * pps + i) * blks + blk, 1), :]
            w = _lane_spread(r, dv)
            prod = w * v_refs[i][0, blk * lanes:(blk + 1) * lanes, :]
            acc = acc + jnp.sum(prod.reshape(pos_per_blk, n_heads, dv), axis=0)
    acc_ref[...] = acc

    @pl.when(j == nj - 1)
    def _():
        o = acc + aself_ref[0] * vs_ref[0]
        ms = jnp.mean(o * o, axis=-1, keepdims=True)
        o_ref[0] = (o * lax.rsqrt(ms + RMS_EPS) * sub_ref[...] * (1.0 - lam_init)).astype(o_ref.dtype)


def _sample_values(pt, at, aself, vs3, sub, cache_v2, *, n_pages, pps, lam_init):
    bd, n_heads, dv = vs3.shape
    n_pool, rows, lanes = cache_v2.shape
    assert lanes == dv and aself.shape[2] == dv

    def page_spec(i):
        return pl.BlockSpec((1, rows, lanes), lambda b, j, pt: (pt[b * n_pages + j * pps + i], 0, 0))

    per = lambda shape: pl.BlockSpec((1,) + tuple(shape[1:]), lambda b, j, pt: (b, 0, 0))
    kern = functools.partial(_sample_values_kernel, pps=pps, lam_init=lam_init)
    return pl.pallas_call(
        kern,
        grid_spec=pltpu.PrefetchScalarGridSpec(
            num_scalar_prefetch=1,
            grid=(bd, n_pages // pps),
            in_specs=[per(at.shape), per(aself.shape), per(vs3.shape),
                      pl.BlockSpec(sub.shape, lambda b, j, pt: (0, 0))]
                     + [page_spec(i) for i in range(pps)],
            out_specs=per((bd, n_heads, dv)),
            scratch_shapes=[pltpu.VMEM((n_heads, dv), F32)],
        ),
        out_shape=jax.ShapeDtypeStruct((bd, n_heads, dv), BF16),
        compiler_params=_params(("arbitrary", "arbitrary")),
        name="sample_values",
    )(pt, at, aself, vs3, sub, *([cache_v2] * pps))


def _oproj_kernel(x_ref, a_ref, w_ref, o_ref):
    o_ref[...] = x_ref[...] + _dot(a_ref[...], w_ref[...])


def _oproj(x, a, w, *, tm):
    t, d = x.shape
    tm = min(tm, t)
    row = pl.BlockSpec((tm, d), lambda i: (i, 0))
    return pl.pallas_call(
        _oproj_kernel,
        grid=(t // tm,),
        in_specs=[row, pl.BlockSpec((tm, a.shape[1]), lambda i: (i, 0)), _const_spec(w.shape)],
        out_specs=row,
        out_shape=jax.ShapeDtypeStruct((t, d), F32),
        compiler_params=_params(("arbitrary",)),
        name="attn_oproj",
    )(x, a, w)


def kernel(x_prompt, x_sample, cache_k, cache_v, page_table, norm_mix, norm_ffn, w_uv, sgu_gain, w_sgu, b_sgu, w_gmlp_o, norm_kv, w_k, w_v, w_q, lam_q, lam_k, subln_gain, w_attn_o, w_router_group, b_router_group, w_router_expert, b_router_expert, w_e_gate, w_e_up, w_e_down, norm_final):
    bsz, seq, d = x_prompt.shape
    dec_b, dec_s, _ = x_sample.shape
    assert dec_s == 1, "sample group handles one new token per sequence"
    n_pool, page, n_heads, n_qk_maps, head_dim = cache_k.shape
    v_dim = cache_v.shape[-1]
    assert n_heads * v_dim == d and n_qk_maps * head_dim == v_dim and n_qk_maps == 2
    assert page == LANES and v_dim == LANES and n_heads == SUBLANES
    n_a = w_uv.shape[0]
    depth = norm_mix.shape[0]
    assert n_a == 1 and depth == 2, "one gMLP layer followed by one attention layer"
    _, n_sgu, chunk, _ = w_sgu.shape
    d_gm = sgu_gain.shape[1]
    sgw = d_gm // n_sgu
    n_groups = w_router_group.shape[2]
    epg = w_router_expert.shape[3]
    n_e = n_groups * epg
    n_pages = page_table.shape[1]
    past_len = n_pages * page
    assert seq % chunk == 0 and seq % page == 0 and seq <= POS_SPLIT * POS_SPLIT

    row2 = lambda a: a.reshape(1, -1)
    wuv = w_uv[0].astype(BF16)
    wo_g = w_gmlp_o[0].astype(BF16)
    wk = w_k.astype(BF16)
    wv = w_v.astype(BF16)
    wq = w_q[0].astype(BF16)
    wao = w_attn_o[0].astype(BF16)
    wg = w_e_gate.astype(BF16)
    wu = w_e_up.astype(BF16)
    wd = w_e_down.astype(BF16)
    wre = jnp.transpose(w_router_expert, (0, 2, 1, 3)).reshape(depth, d, n_e)
    bre = b_router_expert.reshape(depth, 1, n_e)
    brg = b_router_group.reshape(depth, 1, n_groups)
    fin = row2(norm_final)
    slopes = 2.0 ** (-8.0 * jnp.arange(1, n_heads + 1, dtype=F32) / n_heads)
    lam_init = 0.8 - 0.6 * math.exp(-0.3 * 1)
    q_scale = head_dim ** -0.5
    ws_row = jnp.repeat(w_sgu[0, :, 0, 0], sgw).reshape(1, d_gm)
    bs_row = jnp.repeat(b_sgu[0, :, 0], sgw).reshape(1, d_gm)
    bs_t = b_sgu[0].T
    sub = row2(subln_gain[0])
    kp = jnp.arange(seq, dtype=jnp.int32)
    kpos = jnp.zeros((seq, LANES), F32)
    kpos = kpos.at[:, 0].set(1.0).at[:, 1].set(1.0)
    kpos = kpos.at[:, 2].set(((kp // POS_SPLIT) * POS_SPLIT).astype(F32))
    kpos = kpos.at[:, 3].set((kp % POS_SPLIT).astype(F32)).astype(BF16)

    def moe(x, i, final_norm, tm):
        return _moe_dense(x, row2(norm_ffn[i]), w_router_group[i], brg[i], wre[i], bre[i],
                          wg[i], wu[i], wd[i], fin, n_groups=n_groups, epg=epg,
                          final_norm=final_norm, tm=tm)

    tq = min(256, seq)
    xp = x_prompt.reshape(bsz * seq, d)
    (xp,) = _gmlp(xp, row2(norm_mix[0]), wuv, row2(sgu_gain[0]), w_sgu[0], bs_t, wo_g,
                  seq_one=False, chunk=chunk, n_groups=n_sgu, tm=256)
    xp = moe(xp, 0, False, 1024)
    kt_p, v3_p, kb, vbt, qbt = _kvq_prompt(xp, row2(norm_kv), row2(norm_mix[1]), wk, wv, wq,
                                           bsz=bsz, seq=seq, n_heads=n_heads, q_scale=q_scale,
                                           page=page, tk=tq, tm=512)
    att = _attn_prompt(slopes, qbt, kb, kpos, vbt, lam_q[0], lam_k[0], sub,
                       bsz=bsz, seq=seq, n_heads=n_heads, head_dim=head_dim, lam_init=lam_init)
    xp = _oproj(xp, att, wao, tm=512)
    y_prompt = moe(xp, 1, True, 1024).reshape(bsz, seq, d)

    xs = x_sample.reshape(dec_b, d)
    xs, gv = _gmlp(xs, row2(norm_mix[0]), wuv, row2(sgu_gain[0]), ws_row, bs_row, wo_g,
                   seq_one=True, chunk=chunk, n_groups=n_sgu, tm=128)
    xs = moe(xs, 0, False, 128)
    kt_s, k_s, v3_s, q_s = _kvq_sample(xs, row2(norm_kv), row2(norm_mix[1]), wk, wv, wq,
                                       n_heads=n_heads, q_scale=q_scale)
    cache_kt = jnp.transpose(cache_k, (0, 2, 3, 4, 1)).reshape(n_pool, d, page)
    cache_v2 = cache_v.reshape(n_pool, page * n_heads, v_dim)
    pt = page_table.reshape(-1)
    pps = math.gcd(8, n_pages)
    slope_rows = jnp.tile(slopes, n_qk_maps).reshape(n_heads * n_qk_maps, 1)
    s, sself = _sample_scores(pt, q_s.reshape(dec_b, 1, d), k_s.reshape(dec_b, 1, d), slope_rows,
                              cache_kt, n_pages=n_pages, pps=pps)
    a, aself = _sample_probs(s, sself, lam_q[0], lam_k[0], n_heads=n_heads, lam_init=lam_init, sb=8)
    at = jnp.transpose(a, (0, 2, 1)).reshape(dec_b, past_len * n_heads // LANES, LANES)
    att_s = _sample_values(pt, at, aself, v3_s, sub, cache_v2, n_pages=n_pages, pps=pps,
                           lam_init=lam_init)
    xs = _oproj(xs, att_s.reshape(dec_b, d), wao, tm=128)
    y_sample = moe(xs, 1, True, 128).reshape(dec_b, 1, d)

    n_pp = seq // page
    k_prompt = jnp.transpose(kt_p.reshape(bsz, n_pp, n_heads, n_qk_maps, head_dim, page),
                             (0, 1, 5, 2, 3, 4))
    v_prompt = v3_p.reshape(bsz, n_pp, page, n_heads, v_dim)
    k_sample = jnp.transpose(kt_s.reshape(n_heads, n_qk_maps, head_dim, dec_b),
                             (3, 0, 1, 2)).reshape(dec_b, 1, n_heads, n_qk_maps, head_dim)
    v_sample = v3_s.reshape(dec_b, 1, n_heads, v_dim)
    gmlp_v_sample = gv.reshape(n_a, dec_b, 1, d_gm)
    return (y_prompt, y_sample, k_prompt, v_prompt, k_sample, v_sample, gmlp_v_sample)
```

```python
import functools
import math

import jax
import jax.numpy as jnp
from jax import lax
from jax.experimental import pallas as pl
from jax.experimental.pallas import tpu as pltpu

F32 = jnp.float32
BF16 = jnp.bfloat16

RMS_EPS = 1e-6
MASK_VALUE = -1e30
TOP_K_INNER = 2
LANES = 128
SUBLANES = 8
VMEM_LIMIT_BYTES = 56 * 1024 * 1024
POS_SPLIT = 256


def _rms(x, g):
    ms = jnp.mean(x * x, axis=-1, keepdims=True)
    return x * lax.rsqrt(ms + RMS_EPS) * g


def _gelu_tanh(x):
    c = math.sqrt(2.0 / math.pi)
    return x * (0.5 * (1.0 + jnp.tanh(c * (x + 0.044715 * (x * x * x)))))


def _dot(a, b):
    return jnp.dot(a, b, preferred_element_type=F32)


def _params(sem):
    return pltpu.CompilerParams(dimension_semantics=sem, vmem_limit_bytes=VMEM_LIMIT_BYTES)


def _const_spec(shape):
    nd = len(shape)
    return pl.BlockSpec(shape, lambda *_: (0,) * nd)


def _gmlp_kernel(x_ref, nm_ref, wuv_ref, sg_ref, ws_ref, bs_ref, wo_ref, *out_refs,
                 seq_one, chunk, n_groups, slab_groups):
    if seq_one:
        o_ref, v_ref = out_refs
    else:
        (o_ref,) = out_refs
    tm, _ = x_ref.shape
    d_gm = sg_ref.shape[1]
    sgw = d_gm // n_groups
    x = x_ref[...]
    h = _rms(x, nm_ref[...]).astype(BF16)
    v = _gelu_tanh(_dot(h, wuv_ref[:, d_gm:]))
    vn = _rms(v, sg_ref[...])
    if seq_one:
        v_ref[...] = vn
        vb = vn.astype(BF16).astype(F32)
    else:
        vb = vn.astype(BF16)
        row = lax.broadcasted_iota(jnp.int32, (chunk, chunk), 0)
        col = lax.broadcasted_iota(jnp.int32, (chunk, chunk), 1)
        causal = row >= col
    slab = slab_groups * sgw
    acc = x
    for s in range(n_groups // slab_groups):
        c0 = s * slab
        u = _gelu_tanh(_dot(h, wuv_ref[:, c0:c0 + slab]))
        if seq_one:
            gate = vb[:, c0:c0 + slab] * ws_ref[:, c0:c0 + slab] + bs_ref[:, c0:c0 + slab]
        else:
            cols = []
            for gg in range(slab_groups):
                g = s * slab_groups + gg
                w = jnp.where(causal, ws_ref[g], 0.0).astype(BF16)
                bias = jnp.broadcast_to(bs_ref[:, g:g + 1], (chunk, sgw))
                rows = []
                for c in range(tm // chunk):
                    vv = vb[c * chunk:(c + 1) * chunk, g * sgw:(g + 1) * sgw]
                    rows.append(_dot(w, vv) + bias)
                cols.append(jnp.concatenate(rows, axis=0) if len(rows) > 1 else rows[0])
            gate = jnp.concatenate(cols, axis=1) if len(cols) > 1 else cols[0]
        p = (u * gate).astype(BF16)
        acc = acc + _dot(p, wo_ref[c0:c0 + slab, :])
    o_ref[...] = acc


def _gmlp(x, nm, wuv, sg, ws, bs, wo, *, seq_one, chunk, n_groups, tm):
    t, d = x.shape
    d_gm = sg.shape[1]
    tm = min(tm, t)
    slab_groups = 2 if n_groups % 2 == 0 else 1
    out_shape = [jax.ShapeDtypeStruct((t, d), F32)]
    out_specs = [pl.BlockSpec((tm, d), lambda i: (i, 0))]
    if seq_one:
        out_shape.append(jax.ShapeDtypeStruct((t, d_gm), F32))
        out_specs.append(pl.BlockSpec((tm, d_gm), lambda i: (i, 0)))
    kern = functools.partial(_gmlp_kernel, seq_one=seq_one, chunk=chunk, n_groups=n_groups,
                             slab_groups=slab_groups)
    return pl.pallas_call(
        kern,
        grid=(t // tm,),
        in_specs=[pl.BlockSpec((tm, d), lambda i: (i, 0)),
                  _const_spec(nm.shape), _const_spec(wuv.shape), _const_spec(sg.shape),
                  _const_spec(ws.shape), _const_spec(bs.shape), _const_spec(wo.shape)],
        out_specs=out_specs,
        out_shape=out_shape,
        compiler_params=_params(("arbitrary",)),
        name="gmlp_seq1" if seq_one else "gmlp",
    )(x, nm, wuv, sg, ws, bs, wo)


def _router_gate(hf, wrg, brg, wre, bre, n_groups, epg):
    tm = hf.shape[0]
    n_e = n_groups * epg
    lg = jnp.dot(hf, wrg, preferred_element_type=F32, precision=lax.Precision.HIGHEST) + brg
    lane_g = lax.broadcasted_iota(jnp.int32, (tm, n_groups), 1)
    mg = jnp.max(lg, axis=-1, keepdims=True)
    p_grp = 1.0 / jnp.sum(jnp.exp(lg - mg), axis=-1, keepdims=True)
    grp = jnp.min(jnp.where(lg == mg, lane_g, n_groups), axis=-1, keepdims=True)
    el = jnp.dot(hf, wre, preferred_element_type=F32, precision=lax.Precision.HIGHEST) + bre
    lane_e = lax.broadcasted_iota(jnp.int32, (tm, n_e), 1)
    neg = jnp.float32(-jnp.inf)
    in_grp = (lane_e >= grp * epg) & (lane_e < grp * epg + epg)
    el1 = jnp.where(in_grp, el, neg)
    v1 = jnp.max(el1, axis=-1, keepdims=True)
    i1 = jnp.min(jnp.where(el1 == v1, lane_e, n_e), axis=-1, keepdims=True)
    el2 = jnp.where(lane_e == i1, neg, el1)
    v2 = jnp.max(el2, axis=-1, keepdims=True)
    i2 = jnp.min(jnp.where(el2 == v2, lane_e, n_e), axis=-1, keepdims=True)
    e2 = jnp.exp(v2 - v1)
    w1 = 1.0 / (1.0 + e2)
    w2 = e2 / (1.0 + e2)
    gate = jnp.where(lane_e == i1, w1 * p_grp, 0.0) + jnp.where(lane_e == i2, w2 * p_grp, 0.0)
    return gate


def _moe_dense_kernel(x_ref, nf_ref, wrg_ref, brg_ref, wre_ref, bre_ref, wg_ref, wu_ref, wd_ref,
                      fin_ref, o_ref, hb_ref, gate_ref, *, n_groups, epg, final_norm):
    e = pl.program_id(1)
    n_e = n_groups * epg

    @pl.when(e == 0)
    def _():
        x = x_ref[...]
        hf = _rms(x, nf_ref[...])
        hb_ref[...] = hf.astype(BF16)
        gate_ref[...] = _router_gate(hf, wrg_ref[...], brg_ref[...], wre_ref[...], bre_ref[...],
                                     n_groups, epg)
        o_ref[...] = x

    hb = hb_ref[...]
    gate = gate_ref[...]
    lane_e = lax.broadcasted_iota(jnp.int32, gate.shape, 1)
    ge = jnp.sum(jnp.where(lane_e == e, gate, 0.0), axis=-1, keepdims=True)
    a = _dot(hb, wg_ref[0])
    b = _dot(hb, wu_ref[0])
    hid = (a * jax.nn.sigmoid(a)) * b
    o_ref[...] += _dot((hid * ge).astype(BF16), wd_ref[0])

    if final_norm:
        @pl.when(e == n_e - 1)
        def _():
            o_ref[...] = _rms(o_ref[...], fin_ref[...])


def _moe_dense(x, nf, wrg, brg, wre, bre, wg, wu, wd, fin, *, n_groups, epg, final_norm, tm):
    t, d = x.shape
    n_e, _, f = wg.shape
    tm = min(tm, t)
    kern = functools.partial(_moe_dense_kernel, n_groups=n_groups, epg=epg, final_norm=final_norm)
    return pl.pallas_call(
        kern,
        grid=(t // tm, n_e),
        in_specs=[pl.BlockSpec((tm, d), lambda i, e: (i, 0)),
                  _const_spec(nf.shape), _const_spec(wrg.shape), _const_spec(brg.shape),
                  _const_spec(wre.shape), _const_spec(bre.shape),
                  pl.BlockSpec((1, d, f), lambda i, e: (e, 0, 0)),
                  pl.BlockSpec((1, d, f), lambda i, e: (e, 0, 0)),
                  pl.BlockSpec((1, f, d), lambda i, e: (e, 0, 0)),
                  _const_spec(fin.shape)],
        out_specs=pl.BlockSpec((tm, d), lambda i, e: (i, 0)),
        out_shape=jax.ShapeDtypeStruct((t, d), F32),
        scratch_shapes=[pltpu.VMEM((tm, d), BF16), pltpu.VMEM((tm, n_e), F32)],
        compiler_params=_params(("arbitrary", "arbitrary")),
        name="moe_dense",
    )(x, nf, wrg, brg, wre, bre, wg, wu, wd, fin)


def _kvq_prompt_kernel(x_ref, nkv_ref, nq_ref, wk_ref, wv_ref, wq_ref,
                       kt_ref, v3_ref, kb_ref, vbt_ref, qbt_ref, *, q_scale, page):
    tm, d = x_ref.shape
    n_heads, dv = v3_ref.shape[1:]
    x = x_ref[...]
    hk = _rms(x, nkv_ref[...]).astype(BF16)
    k = _dot(hk, wk_ref[...])
    v = _dot(hk, wv_ref[...])
    kb_ref[...] = k.astype(BF16)
    for pg in range(tm // page):
        kt_ref[pg * d:(pg + 1) * d, :] = k[pg * page:(pg + 1) * page, :].T
    for h in range(n_heads):
        v3_ref[:, h, :] = v[:, h * dv:(h + 1) * dv]
    hq = _rms(x, nq_ref[...]).astype(BF16)
    q = _dot(hq, wq_ref[...]) * q_scale
    vbt_ref[0] = v.T.astype(BF16)
    qbt_ref[0] = q.T.astype(BF16)


def _kvq_prompt(x, nkv, nq, wk, wv, wq, *, bsz, seq, n_heads, q_scale, page, tm):
    t, d = x.shape
    dv = d // n_heads
    tm = min(tm, seq)
    nt = seq // tm
    row = pl.BlockSpec((tm, d), lambda i: (i, 0))
    tspec = pl.BlockSpec((1, d, tm), lambda i: (i // nt, 0, i % nt))
    kern = functools.partial(_kvq_prompt_kernel, q_scale=q_scale, page=page)
    return pl.pallas_call(
        kern,
        grid=(t // tm,),
        in_specs=[row, _const_spec(nkv.shape), _const_spec(nq.shape),
                  _const_spec(wk.shape), _const_spec(wv.shape), _const_spec(wq.shape)],
        out_specs=[pl.BlockSpec((tm // page * d, page), lambda i: (i, 0)),
                   pl.BlockSpec((tm, n_heads, dv), lambda i: (i, 0, 0)),
                   row, tspec, tspec],
        out_shape=[jax.ShapeDtypeStruct((t // page * d, page), F32),
                   jax.ShapeDtypeStruct((t, n_heads, dv), F32),
                   jax.ShapeDtypeStruct((t, d), BF16),
                   jax.ShapeDtypeStruct((bsz, d, seq), BF16),
                   jax.ShapeDtypeStruct((bsz, d, seq), BF16)],
        compiler_params=_params(("arbitrary",)),
        name="kvq_prompt",
    )(x, nkv, nq, wk, wv, wq)


def _kvq_sample_kernel(x_ref, nkv_ref, nq_ref, wk_ref, wv_ref, wq_ref,
                       kt_ref, k_ref, v3_ref, q_ref, *, q_scale):
    n_heads, dv = v3_ref.shape[1:]
    x = x_ref[...]
    hk = _rms(x, nkv_ref[...]).astype(BF16)
    k = _dot(hk, wk_ref[...])
    v = _dot(hk, wv_ref[...])
    k_ref[...] = k
    kt_ref[...] = k.T
    for h in range(n_heads):
        v3_ref[:, h, :] = v[:, h * dv:(h + 1) * dv]
    hq = _rms(x, nq_ref[...]).astype(BF16)
    q_ref[...] = _dot(hq, wq_ref[...]) * q_scale


def _kvq_sample(x, nkv, nq, wk, wv, wq, *, n_heads, q_scale):
    t, d = x.shape
    dv = d // n_heads
    kern = functools.partial(_kvq_sample_kernel, q_scale=q_scale)
    return pl.pallas_call(
        kern,
        grid=(1,),
        in_specs=[_const_spec(x.shape), _const_spec(nkv.shape), _const_spec(nq.shape),
                  _const_spec(wk.shape), _const_spec(wv.shape), _const_spec(wq.shape)],
        out_specs=[_const_spec((d, t)), _const_spec((t, d)), _const_spec((t, n_heads, dv)),
                   _const_spec((t, d))],
        out_shape=[jax.ShapeDtypeStruct((d, t), F32), jax.ShapeDtypeStruct((t, d), F32),
                   jax.ShapeDtypeStruct((t, n_heads, dv), F32), jax.ShapeDtypeStruct((t, d), F32)],
        compiler_params=_params(("arbitrary",)),
        name="kvq_sample",
    )(x, nkv, nq, wk, wv, wq)


def _lam_value(lq_ref, lk_ref, lam_init):
    lq = lq_ref[...]
    lk = lk_ref[...]
    s = jnp.sum(lq * lk, axis=-1, keepdims=True)
    return jnp.exp(s[0:1]) - jnp.exp(s[1:2]) + lam_init


def _attn_prompt_kernel(slopes_ref, qt_ref, k_ref, kpos_ref, vt_ref, lq_ref, lk_ref, sub_ref, o_ref,
                        *, head_dim, lam_init, tq):
    h = pl.program_id(1)
    dv, seq = qt_ref.shape[1:]
    nf = kpos_ref.shape[1]
    slope = slopes_ref[h]
    lam = _lam_value(lq_ref, lk_ref, lam_init)
    row = lax.broadcasted_iota(jnp.int32, (dv, tq), 0)
    fr = lax.broadcasted_iota(jnp.int32, (SUBLANES, 2 * tq), 0)
    ci = lax.broadcasted_iota(jnp.int32, (SUBLANES, 2 * tq), 1)
    qcol = jnp.where(ci >= tq, ci - tq, ci)
    kr = lax.broadcasted_iota(jnp.int32, (tq, 2 * tq), 0)
    kc = lax.broadcasted_iota(jnp.int32, (tq, 2 * tq), 1)
    visible = jnp.where(kc >= tq, kc - tq, kc) >= kr

    for i in range(seq // tq):
        kv = (i + 1) * tq
        qt = qt_ref[0, :, i * tq:(i + 1) * tq].astype(F32)
        qq = jnp.concatenate([jnp.where(row < head_dim, qt, 0.0),
                              jnp.where(row >= head_dim, qt, 0.0)], axis=1)
        ipos = i * tq + qcol
        ihi = ((ipos // POS_SPLIT) * POS_SPLIT).astype(F32)
        ilo = (ipos % POS_SPLIT).astype(F32)
        feat = jnp.where(fr == 0, -slope * ihi,
                         jnp.where(fr == 1, -slope * ilo,
                                   jnp.where((fr == 2) | (fr == 3), slope, 0.0)))
        qqa = jnp.concatenate([qq, feat, jnp.zeros((nf - SUBLANES, 2 * tq), F32)],
                              axis=0).astype(BF16)
        ka = jnp.concatenate([k_ref[:kv, :], kpos_ref[:kv, :]], axis=1)
        s = _dot(ka, qqa)
        diag = jnp.where(visible, s[kv - tq:, :], MASK_VALUE)
        s = jnp.concatenate([s[:kv - tq, :], diag], axis=0) if i > 0 else diag
        m = jnp.max(s, axis=0, keepdims=True)
        p = jnp.exp(s - m)
        l = jnp.sum(p, axis=0, keepdims=True)
        o2 = _dot(vt_ref[0, :, :kv], p.astype(BF16)) / l
        ot = o2[:, :tq] - lam * o2[:, tq:]
        ms = jnp.mean(ot * ot, axis=0, keepdims=True)
        ot = ot * lax.rsqrt(ms + RMS_EPS)
        o_ref[i * tq:(i + 1) * tq, :] = (ot.T * sub_ref[...] * (1.0 - lam_init)).astype(o_ref.dtype)


def _attn_prompt(slopes, qbt, kb, kpos, vbt, lq, lk, sub, *, bsz, seq, n_heads, head_dim, lam_init, tq):
    _, d, _ = qbt.shape
    dv = d // n_heads
    kern = functools.partial(_attn_prompt_kernel, head_dim=head_dim, lam_init=lam_init, tq=tq)
    tspec = pl.BlockSpec((1, dv, seq), lambda b, h: (b, h, 0))
    return pl.pallas_call(
        kern,
        grid=(bsz, n_heads),
        in_specs=[pl.BlockSpec(memory_space=pltpu.SMEM),
                  tspec,
                  pl.BlockSpec((seq, dv), lambda b, h: (b, h)),
                  pl.BlockSpec(kpos.shape, lambda b, h: (0, 0)),
                  tspec,
                  pl.BlockSpec(lq.shape, lambda b, h: (0, 0)),
                  pl.BlockSpec(lk.shape, lambda b, h: (0, 0)),
                  pl.BlockSpec(sub.shape, lambda b, h: (0, 0))],
        out_specs=pl.BlockSpec((seq, dv), lambda b, h: (b, h)),
        out_shape=jax.ShapeDtypeStruct((bsz * seq, d), BF16),
        compiler_params=_params(("arbitrary", "arbitrary")),
        name="attn_prompt",
    )(slopes, qbt, kb, kpos, vbt, lq, lk, sub)


def _lane_spread(row, n):
    return jnp.broadcast_to(row, (n, row.shape[1])).T


def _sample_scores_kernel(pt_ref, q_ref, ks_ref, slope_ref, *refs, pps, n_maps, past_len):
    k_refs = refs[:pps]
    s_ref, sself_ref, qb_ref = refs[pps:]
    j = pl.program_id(1)
    d, page = qb_ref.shape
    n_heads = n_maps // 2
    hd = d // n_maps

    def map_sums(t):
        t3 = t.reshape(n_heads, 2 * hd, page)
        return jnp.concatenate([jnp.sum(t3[:, :hd, :], axis=1), jnp.sum(t3[:, hd:, :], axis=1)],
                               axis=0)

    @pl.when(j == 0)
    def _():
        qb = _lane_spread(q_ref[0], page)
        qb_ref[...] = qb
        sself_ref[0] = map_sums(qb * _lane_spread(ks_ref[0], page))

    qb = qb_ref[...]
    slope = slope_ref[...]
    lane = lax.broadcasted_iota(jnp.int32, (n_maps, page), 1)
    for i in range(pps):
        s = map_sums(k_refs[i][0] * qb)
        dist = (past_len - (j * pps + i) * page) - lane
        s_ref[0, :, i * page:(i + 1) * page] = s - slope * dist.astype(F32)


def _sample_scores(pt, q3, ks3, slope_rows, cache_kt, *, n_pages, pps):
    bd, _, d = q3.shape
    n_pool, _, page = cache_kt.shape
    n_maps = slope_rows.shape[0]
    past_len = n_pages * page
    tok = pl.BlockSpec((1, 1, d), lambda b, j, pt: (b, 0, 0))

    def page_spec(i):
        return pl.BlockSpec((1, d, page), lambda b, j, pt: (pt[b * n_pages + j * pps + i], 0, 0))

    kern = functools.partial(_sample_scores_kernel, pps=pps, n_maps=n_maps, past_len=past_len)
    return pl.pallas_call(
        kern,
        grid_spec=pltpu.PrefetchScalarGridSpec(
            num_scalar_prefetch=1,
            grid=(bd, n_pages // pps),
            in_specs=[tok, tok, pl.BlockSpec(slope_rows.shape, lambda b, j, pt: (0, 0))]
                     + [page_spec(i) for i in range(pps)],
            out_specs=[pl.BlockSpec((1, n_maps, pps * page), lambda b, j, pt: (b, 0, j)),
                       pl.BlockSpec((1, n_maps, page), lambda b, j, pt: (b, 0, 0))],
            scratch_shapes=[pltpu.VMEM((d, page), F32)],
        ),
        out_shape=[jax.ShapeDtypeStruct((bd, n_maps, past_len), F32),
                   jax.ShapeDtypeStruct((bd, n_maps, page), F32)],
        compiler_params=_params(("arbitrary", "arbitrary")),
        name="sample_scores",
    )(pt, q3, ks3, slope_rows, *([cache_kt] * pps))


def _sample_probs_kernel(s_ref, sself_ref, lq_ref, lk_ref, a_ref, aself_ref, *, n_heads, lam_init):
    lam = _lam_value(lq_ref, lk_ref, lam_init)

    def normalized(mp):
        s = s_ref[:, mp * n_heads:(mp + 1) * n_heads, :]
        ss = sself_ref[:, mp * n_heads:(mp + 1) * n_heads, :]
        m = jnp.maximum(jnp.max(s, axis=-1, keepdims=True), ss[:, :, :1])
        p = jnp.exp(s - m)
        ps = jnp.exp(ss - m)
        l = jnp.sum(p, axis=-1, keepdims=True) + ps[:, :, :1]
        return p / l, ps / l

    p0, ps0 = normalized(0)
    p1, ps1 = normalized(1)
    a_ref[...] = p0 - lam * p1
    aself_ref[...] = ps0 - lam * ps1


def _sample_probs(s, sself, lq, lk, *, n_heads, lam_init, sb):
    bd, n_maps, past = s.shape
    lanes = sself.shape[2]
    sb = math.gcd(sb, bd)
    kern = functools.partial(_sample_probs_kernel, n_heads=n_heads, lam_init=lam_init)
    return pl.pallas_call(
        kern,
        grid=(bd // sb,),
        in_specs=[pl.BlockSpec((sb, n_maps, past), lambda i: (i, 0, 0)),
                  pl.BlockSpec((sb, n_maps, lanes), lambda i: (i, 0, 0)),
                  _const_spec(lq.shape), _const_spec(lk.shape)],
        out_specs=[pl.BlockSpec((sb, n_heads, past), lambda i: (i, 0, 0)),
                   pl.BlockSpec((sb, n_heads, lanes), lambda i: (i, 0, 0))],
        out_shape=[jax.ShapeDtypeStruct((bd, n_heads, past), F32),
                   jax.ShapeDtypeStruct((bd, n_heads, lanes), F32)],
        compiler_params=_params(("arbitrary",)),
        name="sample_probs",
    )(s, sself, lq, lk)


def _sample_values_kernel(pt_ref, at_ref, aself_ref, vs_ref, sub_ref, *refs, pps, lam_init):
    v_refs = refs[:pps]
    o_ref, acc_ref = refs[pps:]
    j = pl.program_id(1)
    nj = pl.num_programs(1)
    n_heads, dv = acc_ref.shape
    rows, lanes = v_refs[0].shape[1:]
    pos_per_blk = lanes // n_heads
    blks = rows // lanes

    @pl.when(j == 0)
    def _():
        acc_ref[...] = jnp.zeros(acc_ref.shape, F32)

    acc = acc_ref[...]
    for i in range(pps):
        for blk in range(blks):
            r = at_ref[0, pl.ds((j * pps + i) * blks + blk, 1), :]
            w = _lane_spread(r, dv)
            prod = w * v_refs[i][0, blk * lanes:(blk + 1) * lanes, :]
            acc = acc + jnp.sum(prod.reshape(pos_per_blk, n_heads, dv), axis=0)
    acc_ref[...] = acc

    @pl.when(j == nj - 1)
    def _():
        o = acc + aself_ref[0] * vs_ref[0]
        ms = jnp.mean(o * o, axis=-1, keepdims=True)
        o_ref[0] = (o * lax.rsqrt(ms + RMS_EPS) * sub_ref[...] * (1.0 - lam_init)).astype(o_ref.dtype)


def _sample_values(pt, at, aself, vs3, sub, cache_v2, *, n_pages, pps, lam_init):
    bd, n_heads, dv = vs3.shape
    n_pool, rows, lanes = cache_v2.shape
    assert lanes == dv and aself.shape[2] == dv

    def page_spec(i):
        return pl.BlockSpec((1, rows, lanes), lambda b, j, pt: (pt[b * n_pages + j * pps + i], 0, 0))

    per = lambda shape: pl.BlockSpec((1,) + tuple(shape[1:]), lambda b, j, pt: (b, 0, 0))
    kern = functools.partial(_sample_values_kernel, pps=pps, lam_init=lam_init)
    return pl.pallas_call(
        kern,
        grid_spec=pltpu.PrefetchScalarGridSpec(
            num_scalar_prefetch=1,
            grid=(bd, n_pages // pps),
            in_specs=[per(at.shape), per(aself.shape), per(vs3.shape),
                      pl.BlockSpec(sub.shape, lambda b, j, pt: (0, 0))]
                     + [page_spec(i) for i in range(pps)],
            out_specs=per((bd, n_heads, dv)),
            scratch_shapes=[pltpu.VMEM((n_heads, dv), F32)],
        ),
        out_shape=jax.ShapeDtypeStruct((bd, n_heads, dv), BF16),
        compiler_params=_params(("arbitrary", "arbitrary")),
        name="sample_values",
    )(pt, at, aself, vs3, sub, *([cache_v2] * pps))


def _oproj_kernel(x_ref, a_ref, w_ref, o_ref):
    o_ref[...] = x_ref[...] + _dot(a_ref[...], w_ref[...])


def _oproj(x, a, w, *, tm):
    t, d = x.shape
    tm = min(tm, t)
    row = pl.BlockSpec((tm, d), lambda i: (i, 0))
    return pl.pallas_call(
        _oproj_kernel,
        grid=(t // tm,),
        in_specs=[row, pl.BlockSpec((tm, a.shape[1]), lambda i: (i, 0)), _const_spec(w.shape)],
        out_specs=row,
        out_shape=jax.ShapeDtypeStruct((t, d), F32),
        compiler_params=_params(("arbitrary",)),
        name="attn_oproj",
    )(x, a, w)


def kernel(x_prompt, x_sample, cache_k, cache_v, page_table, norm_mix, norm_ffn, w_uv, sgu_gain, w_sgu, b_sgu, w_gmlp_o, norm_kv, w_k, w_v, w_q, lam_q, lam_k, subln_gain, w_attn_o, w_router_group, b_router_group, w_router_expert, b_router_expert, w_e_gate, w_e_up, w_e_down, norm_final):
    bsz, seq, d = x_prompt.shape
    dec_b, dec_s, _ = x_sample.shape
    assert dec_s == 1, "sample group handles one new token per sequence"
    n_pool, page, n_heads, n_qk_maps, head_dim = cache_k.shape
    v_dim = cache_v.shape[-1]
    assert n_heads * v_dim == d and n_qk_maps * head_dim == v_dim and n_qk_maps == 2
    assert page == LANES and v_dim == LANES and n_heads == SUBLANES
    n_a = w_uv.shape[0]
    depth = norm_mix.shape[0]
    assert n_a == 1 and depth == 2, "one gMLP layer followed by one attention layer"
    _, n_sgu, chunk, _ = w_sgu.shape
    d_gm = sgu_gain.shape[1]
    sgw = d_gm // n_sgu
    n_groups = w_router_group.shape[2]
    epg = w_router_expert.shape[3]
    n_e = n_groups * epg
    n_pages = page_table.shape[1]
    past_len = n_pages * page
    assert seq % chunk == 0 and seq % page == 0 and seq <= POS_SPLIT * POS_SPLIT

    row2 = lambda a: a.reshape(1, -1)
    wuv = w_uv[0].astype(BF16)
    wo_g = w_gmlp_o[0].astype(BF16)
    wk = w_k.astype(BF16)
    wv = w_v.astype(BF16)
    wq = w_q[0].astype(BF16)
    wao = w_attn_o[0].astype(BF16)
    wg = w_e_gate.astype(BF16)
    wu = w_e_up.astype(BF16)
    wd = w_e_down.astype(BF16)
    wre = jnp.transpose(w_router_expert, (0, 2, 1, 3)).reshape(depth, d, n_e)
    bre = b_router_expert.reshape(depth, 1, n_e)
    brg = b_router_group.reshape(depth, 1, n_groups)
    fin = row2(norm_final)
    slopes = 2.0 ** (-8.0 * jnp.arange(1, n_heads + 1, dtype=F32) / n_heads)
    lam_init = 0.8 - 0.6 * math.exp(-0.3 * 1)
    q_scale = head_dim ** -0.5
    ws_row = jnp.repeat(w_sgu[0, :, 0, 0], sgw).reshape(1, d_gm)
    bs_row = jnp.repeat(b_sgu[0, :, 0], sgw).reshape(1, d_gm)
    bs_t = b_sgu[0].T
    sub = row2(subln_gain[0])
    kp = jnp.arange(seq, dtype=jnp.int32)
    kpos = jnp.zeros((seq, LANES), F32)
    kpos = kpos.at[:, 0].set(1.0).at[:, 1].set(1.0)
    kpos = kpos.at[:, 2].set(((kp // POS_SPLIT) * POS_SPLIT).astype(F32))
    kpos = kpos.at[:, 3].set((kp % POS_SPLIT).astype(F32)).astype(BF16)

    def moe(x, i, final_norm, tm):
        return _moe_dense(x, row2(norm_ffn[i]), w_router_group[i], brg[i], wre[i], bre[i],
                          wg[i], wu[i], wd[i], fin, n_groups=n_groups, epg=epg,
                          final_norm=final_norm, tm=tm)

    tq = min(256, seq)
    xp = x_prompt.reshape(bsz * seq, d)
    (xp,) = _gmlp(xp, row2(norm_mix[0]), wuv, row2(sgu_gain[0]), w_sgu[0], bs_t, wo_g,
                  seq_one=False, chunk=chunk, n_groups=n_sgu, tm=256)
    xp = moe(xp, 0, False, 1024)
    kt_p, v3_p, kb, vbt, qbt = _kvq_prompt(xp, row2(norm_kv), row2(norm_mix[1]), wk, wv, wq,
                                           bsz=bsz, seq=seq, n_heads=n_heads, q_scale=q_scale,
                                           page=page, tm=512)
    att = _attn_prompt(slopes, qbt, kb, kpos, vbt, lam_q[0], lam_k[0], sub,
                       bsz=bsz, seq=seq, n_heads=n_heads, head_dim=head_dim, lam_init=lam_init,
                       tq=tq)
    xp = _oproj(xp, att, wao, tm=512)
    y_prompt = moe(xp, 1, True, 1024).reshape(bsz, seq, d)

    xs = x_sample.reshape(dec_b, d)
    xs, gv = _gmlp(xs, row2(norm_mix[0]), wuv, row2(sgu_gain[0]), ws_row, bs_row, wo_g,
                   seq_one=True, chunk=chunk, n_groups=n_sgu, tm=128)
    xs = moe(xs, 0, False, 128)
    kt_s, k_s, v3_s, q_s = _kvq_sample(xs, row2(norm_kv), row2(norm_mix[1]), wk, wv, wq,
                                       n_heads=n_heads, q_scale=q_scale)
    cache_kt = jnp.transpose(cache_k, (0, 2, 3, 4, 1)).reshape(n_pool, d, page)
    cache_v2 = cache_v.reshape(n_pool, page * n_heads, v_dim)
    pt = page_table.reshape(-1)
    pps = math.gcd(8, n_pages)
    slope_rows = jnp.tile(slopes, n_qk_maps).reshape(n_heads * n_qk_maps, 1)
    s, sself = _sample_scores(pt, q_s.reshape(dec_b, 1, d), k_s.reshape(dec_b, 1, d), slope_rows,
                              cache_kt, n_pages=n_pages, pps=pps)
    a, aself = _sample_probs(s, sself, lam_q[0], lam_k[0], n_heads=n_heads, lam_init=lam_init, sb=8)
    at = jnp.transpose(a, (0, 2, 1)).reshape(dec_b, past_len * n_heads // LANES, LANES)
    att_s = _sample_values(pt, at, aself, v3_s, sub, cache_v2, n_pages=n_pages, pps=pps,
                           lam_init=lam_init)
    xs = _oproj(xs, att_s.reshape(dec_b, d), wao, tm=128)
    y_sample = moe(xs, 1, True, 128).reshape(dec_b, 1, d)

    n_pp = seq // page
    k_prompt = jnp.transpose(kt_p.reshape(bsz, n_pp, n_heads, n_qk_maps, head_dim, page),
                             (0, 1, 5, 2, 3, 4))
    v_prompt = v3_p.reshape(bsz, n_pp, page, n_heads, v_dim)
    k_sample = jnp.transpose(kt_s.reshape(n_heads, n_qk_maps, head_dim, dec_b),
                             (3, 0, 1, 2)).reshape(dec_b, 1, n_heads, n_qk_maps, head_dim)
    v_sample = v3_s.reshape(dec_b, 1, n_heads, v_dim)
    gmlp_v_sample = gv.reshape(n_a, dec_b, 1, d_gm)
    return (y_prompt, y_sample, k_prompt, v_prompt, k_sample, v_sample, gmlp_v_sample)
```

```python
import functools
import math

import jax
import jax.numpy as jnp
from jax import lax
from jax.experimental import pallas as pl
from jax.experimental.pallas import tpu as pltpu

F32 = jnp.float32
BF16 = jnp.bfloat16

RMS_EPS = 1e-6
MASK_VALUE = -1e30
TOP_K_INNER = 2
LANES = 128
SUBLANES = 8
VMEM_LIMIT_BYTES = 56 * 1024 * 1024
POS_SPLIT = 256


def _rms(x, g):
    ms = jnp.mean(x * x, axis=-1, keepdims=True)
    return x * lax.rsqrt(ms + RMS_EPS) * g


def _gelu_tanh(x):
    c = math.sqrt(2.0 / math.pi)
    return x * (0.5 * (1.0 + jnp.tanh(c * (x + 0.044715 * (x * x * x)))))


def _dot(a, b):
    return jnp.dot(a, b, preferred_element_type=F32)


def _params(sem):
    return pltpu.CompilerParams(dimension_semantics=sem, vmem_limit_bytes=VMEM_LIMIT_BYTES)


def _const_spec(shape):
    nd = len(shape)
    return pl.BlockSpec(shape, lambda *_: (0,) * nd)


def _gmlp_kernel(x_ref, nm_ref, wuv_ref, sg_ref, ws_ref, bs_ref, wo_ref, *out_refs,
                 seq_one, chunk, n_groups, slab_groups):
    if seq_one:
        o_ref, v_ref = out_refs
    else:
        (o_ref,) = out_refs
    tm, _ = x_ref.shape
    d_gm = sg_ref.shape[1]
    sgw = d_gm // n_groups
    x = x_ref[...]
    h = _rms(x, nm_ref[...]).astype(BF16)
    v = _gelu_tanh(_dot(h, wuv_ref[:, d_gm:]))
    vn = _rms(v, sg_ref[...])
    if seq_one:
        v_ref[...] = vn
        vb = vn.astype(BF16).astype(F32)
    else:
        vb = vn.astype(BF16)
        row = lax.broadcasted_iota(jnp.int32, (chunk, chunk), 0)
        col = lax.broadcasted_iota(jnp.int32, (chunk, chunk), 1)
        causal = row >= col
    slab = slab_groups * sgw
    acc = x
    for s in range(n_groups // slab_groups):
        c0 = s * slab
        u = _gelu_tanh(_dot(h, wuv_ref[:, c0:c0 + slab]))
        if seq_one:
            gate = vb[:, c0:c0 + slab] * ws_ref[:, c0:c0 + slab] + bs_ref[:, c0:c0 + slab]
        else:
            cols = []
            for gg in range(slab_groups):
                g = s * slab_groups + gg
                w = jnp.where(causal, ws_ref[g], 0.0).astype(BF16)
                bias = jnp.broadcast_to(bs_ref[:, g:g + 1], (chunk, sgw))
                rows = []
                for c in range(tm // chunk):
                    vv = vb[c * chunk:(c + 1) * chunk, g * sgw:(g + 1) * sgw]
                    rows.append(_dot(w, vv) + bias)
                cols.append(jnp.concatenate(rows, axis=0) if len(rows) > 1 else rows[0])
            gate = jnp.concatenate(cols, axis=1) if len(cols) > 1 else cols[0]
        p = (u * gate).astype(BF16)
        acc = acc + _dot(p, wo_ref[c0:c0 + slab, :])
    o_ref[...] = acc


def _gmlp(x, nm, wuv, sg, ws, bs, wo, *, seq_one, chunk, n_groups, tm):
    t, d = x.shape
    d_gm = sg.shape[1]
    tm = min(tm, t)
    slab_groups = 2 if n_groups % 2 == 0 else 1
    out_shape = [jax.ShapeDtypeStruct((t, d), F32)]
    out_specs = [pl.BlockSpec((tm, d), lambda i: (i, 0))]
    if seq_one:
        out_shape.append(jax.ShapeDtypeStruct((t, d_gm), F32))
        out_specs.append(pl.BlockSpec((tm, d_gm), lambda i: (i, 0)))
    kern = functools.partial(_gmlp_kernel, seq_one=seq_one, chunk=chunk, n_groups=n_groups,
                             slab_groups=slab_groups)
    return pl.pallas_call(
        kern,
        grid=(t // tm,),
        in_specs=[pl.BlockSpec((tm, d), lambda i: (i, 0)),
                  _const_spec(nm.shape), _const_spec(wuv.shape), _const_spec(sg.shape),
                  _const_spec(ws.shape), _const_spec(bs.shape), _const_spec(wo.shape)],
        out_specs=out_specs,
        out_shape=out_shape,
        compiler_params=_params(("arbitrary",)),
        name="gmlp_seq1" if seq_one else "gmlp",
    )(x, nm, wuv, sg, ws, bs, wo)


def _dot_split(a, b):
    a_hi = a.astype(BF16)
    a_lo = (a - a_hi.astype(F32)).astype(BF16)
    b_hi = b.astype(BF16)
    b_lo = (b - b_hi.astype(F32)).astype(BF16)
    return _dot(a_hi, b_hi) + _dot(a_lo, b_hi) + _dot(a_hi, b_lo)


def _router_top2(hf, wr, br, n_groups, epg):
    n_e = n_groups * epg
    logits = _dot_split(hf, wr) + br
    lane = lax.broadcasted_iota(jnp.int32, logits.shape, 1)
    neg = jnp.float32(-jnp.inf)
    is_grp = lane >= n_e
    mg = jnp.max(jnp.where(is_grp, logits, neg), axis=-1, keepdims=True)
    p_grp = 1.0 / jnp.sum(jnp.where(is_grp, jnp.exp(logits - mg), 0.0), axis=-1, keepdims=True)
    grp = jnp.min(jnp.where(is_grp & (logits == mg), lane - n_e, n_groups), axis=-1, keepdims=True)
    in_grp = (lane >= grp * epg) & (lane < grp * epg + epg)
    el1 = jnp.where(in_grp, logits, neg)
    v1 = jnp.max(el1, axis=-1, keepdims=True)
    i1 = jnp.min(jnp.where(el1 == v1, lane, n_e), axis=-1, keepdims=True)
    el2 = jnp.where(lane == i1, neg, el1)
    v2 = jnp.max(el2, axis=-1, keepdims=True)
    i2 = jnp.min(jnp.where(el2 == v2, lane, n_e), axis=-1, keepdims=True)
    e2 = jnp.exp(v2 - v1)
    w1 = 1.0 / (1.0 + e2)
    w2 = e2 / (1.0 + e2)
    return grp, i1, i2, w1 * p_grp, w2 * p_grp


def _router_gate(hf, wr, br, n_groups, epg):
    _, i1, i2, g1, g2 = _router_top2(hf, wr, br, n_groups, epg)
    lane_e = lax.broadcasted_iota(jnp.int32, (hf.shape[0], n_groups * epg), 1)
    return jnp.where(lane_e == i1, g1, 0.0) + jnp.where(lane_e == i2, g2, 0.0)


def _moe_dense_kernel(x_ref, nf_ref, wr_ref, br_ref, wg_ref, wu_ref, wd_ref,
                      fin_ref, o_ref, hb_ref, gate_ref, *, n_groups, epg, final_norm):
    e = pl.program_id(1)
    n_e = n_groups * epg

    @pl.when(e == 0)
    def _():
        x = x_ref[...]
        hf = _rms(x, nf_ref[...])
        hb_ref[...] = hf.astype(BF16)
        gate_ref[...] = _router_gate(hf, wr_ref[...], br_ref[...], n_groups, epg)
        o_ref[...] = x

    hb = hb_ref[...]
    gate = gate_ref[...]
    lane_e = lax.broadcasted_iota(jnp.int32, gate.shape, 1)
    ge = jnp.sum(jnp.where(lane_e == e, gate, 0.0), axis=-1, keepdims=True)
    a = _dot(hb, wg_ref[0])
    b = _dot(hb, wu_ref[0])
    hid = (a * jax.nn.sigmoid(a)) * b
    o_ref[...] += _dot((hid * ge).astype(BF16), wd_ref[0])

    if final_norm:
        @pl.when(e == n_e - 1)
        def _():
            o_ref[...] = _rms(o_ref[...], fin_ref[...])


def _moe_dense(x, nf, wr, br, wg, wu, wd, fin, *, n_groups, epg, final_norm, tm):
    t, d = x.shape
    n_e, _, f = wg.shape
    tm = min(tm, t)
    kern = functools.partial(_moe_dense_kernel, n_groups=n_groups, epg=epg, final_norm=final_norm)
    return pl.pallas_call(
        kern,
        grid=(t // tm, n_e),
        in_specs=[pl.BlockSpec((tm, d), lambda i, e: (i, 0)),
                  _const_spec(nf.shape), _const_spec(wr.shape), _const_spec(br.shape),
                  pl.BlockSpec((1, d, f), lambda i, e: (e, 0, 0)),
                  pl.BlockSpec((1, d, f), lambda i, e: (e, 0, 0)),
                  pl.BlockSpec((1, f, d), lambda i, e: (e, 0, 0)),
                  _const_spec(fin.shape)],
        out_specs=pl.BlockSpec((tm, d), lambda i, e: (i, 0)),
        out_shape=jax.ShapeDtypeStruct((t, d), F32),
        scratch_shapes=[pltpu.VMEM((tm, d), BF16), pltpu.VMEM((tm, n_e), F32)],
        compiler_params=_params(("arbitrary", "arbitrary")),
        name="moe_dense",
    )(x, nf, wr, br, wg, wu, wd, fin)


EXPERT_TILE_ROWS = 256
ROW_DMA_UNROLL = 8


def _route_kernel(x_ref, nf_ref, wr_ref, br_ref, xa_ref, cnt_ref, carry_ref, *, n_groups, epg):
    i = pl.program_id(0)
    tm, d = x_ref.shape
    lanes = cnt_ref.shape[1]
    n_pair = epg * (epg - 1) // 2

    @pl.when(i == 0)
    def _():
        carry_ref[...] = jnp.zeros(carry_ref.shape, F32)

    x = x_ref[...]
    hf = _rms(x, nf_ref[...])
    grp, i1, i2, g1, g2 = _router_top2(hf, wr_ref[...], br_ref[...], n_groups, epg)
    first = i1 < i2
    la = jnp.where(first, i1, i2) - grp * epg
    lb = jnp.where(first, i2, i1) - grp * epg
    ga = jnp.where(first, g1, g2)
    gb = jnp.where(first, g2, g1)
    pair = (la * (2 * epg - 1 - la)) // 2 + (lb - la - 1)
    bucket = grp * n_pair + pair
    lane = lax.broadcasted_iota(jnp.int32, (tm, lanes), 1)
    onehot = jnp.where(lane == bucket, 1.0, 0.0)
    r_id = lax.broadcasted_iota(jnp.int32, (tm, tm), 0)
    c_id = lax.broadcasted_iota(jnp.int32, (tm, tm), 1)
    earlier = jnp.where(r_id > c_id, 1.0, 0.0).astype(BF16)
    prefix = _dot(earlier, onehot.astype(BF16))
    carry = carry_ref[...]
    rank = jnp.sum(onehot * (prefix + carry), axis=-1, keepdims=True)
    carry = carry + jnp.sum(onehot, axis=0, keepdims=True)
    carry_ref[...] = carry
    cnt_ref[...] = carry
    xa_ref[:, :d] = x
    xa_ref[:, d:] = jnp.where(lane == 0, ga,
                              jnp.where(lane == 1, gb,
                                        jnp.where(lane == 2, bucket.astype(F32),
                                                  jnp.where(lane == 3, rank, 0.0))))


def _route(x, nf, wr, br, *, n_groups, epg, tm):
    t, d = x.shape
    tm = min(tm, t)
    kern = functools.partial(_route_kernel, n_groups=n_groups, epg=epg)
    return pl.pallas_call(
        kern,
        grid=(t // tm,),
        in_specs=[pl.BlockSpec((tm, d), lambda i: (i, 0)),
                  _const_spec(nf.shape), _const_spec(wr.shape), _const_spec(br.shape)],
        out_specs=[pl.BlockSpec((tm, d + LANES), lambda i: (i, 0)), _const_spec((1, LANES))],
        out_shape=[jax.ShapeDtypeStruct((t, d + LANES), F32), jax.ShapeDtypeStruct((1, LANES), F32)],
        scratch_shapes=[pltpu.VMEM((1, LANES), F32)],
        compiler_params=_params(("arbitrary",)),
        name="moe_route",
    )(x, nf, wr, br)


def _row_copy(src, src_row, dst, dst_row, sem):
    return pltpu.make_async_copy(src.at[pl.ds(src_row, 1), :], dst.at[pl.ds(dst_row, 1), :], sem)


def _dispatch_kernel(dest_ref, xa_ref, xs_in_ref, xs_ref, sem):
    del xs_in_ref
    tm = xa_ref.shape[0]
    base = pl.program_id(0) * tm

    def start(r, c):
        _row_copy(xa_ref, r, xs_ref, dest_ref[base + r], sem).start()
        return c

    def wait(r, c):
        _row_copy(xa_ref, r, xs_ref, dest_ref[base + r], sem).wait()
        return c

    lax.fori_loop(0, tm, start, 0, unroll=ROW_DMA_UNROLL)
    lax.fori_loop(0, tm, wait, 0, unroll=ROW_DMA_UNROLL)


def _dispatch(dest, xa, n_rows, *, tm):
    t, xw = xa.shape
    tm = min(tm, t)
    xs0 = jnp.zeros((n_rows, xw), F32)
    return pl.pallas_call(
        _dispatch_kernel,
        grid_spec=pltpu.PrefetchScalarGridSpec(
            num_scalar_prefetch=1,
            grid=(t // tm,),
            in_specs=[pl.BlockSpec((tm, xw), lambda i, dest: (i, 0)),
                      pl.BlockSpec(memory_space=pl.ANY)],
            out_specs=pl.BlockSpec(memory_space=pl.ANY),
            scratch_shapes=[pltpu.SemaphoreType.DMA],
        ),
        out_shape=jax.ShapeDtypeStruct((n_rows, xw), F32),
        input_output_aliases={2: 0},
        compiler_params=_params(("arbitrary",)),
        name="moe_dispatch",
    )(dest, xa, xs0)


def _expert_kernel(ea_ref, eb_ref, valid_ref, xs_ref, nf_ref, wga_ref, wgb_ref, wua_ref, wub_ref,
                   wda_ref, wdb_ref, ys_ref):
    i = pl.program_id(0)
    d = ys_ref.shape[1]

    @pl.when(valid_ref[i] != 0)
    def _():
        x = xs_ref[:, :d]
        gates = xs_ref[:, d:]
        h = _rms(x, nf_ref[...]).astype(BF16)

        def expert(wg_ref, wu_ref, wd_ref, gate):
            a = _dot(h, wg_ref[0])
            b = _dot(h, wu_ref[0])
            hid = (a * jax.nn.sigmoid(a)) * b
            return _dot((hid * gate).astype(BF16), wd_ref[0])

        ys_ref[...] = (x + expert(wga_ref, wua_ref, wda_ref, gates[:, 0:1])
                       + expert(wgb_ref, wub_ref, wdb_ref, gates[:, 1:2]))

    @pl.when(valid_ref[i] == 0)
    def _():
        ys_ref[...] = jnp.zeros(ys_ref.shape, F32)


def _experts(tile_ea, tile_eb, tile_valid, xs, nf, wg, wu, wd):
    n_rows, xw = xs.shape
    n_e, d, f = wg.shape
    rt = EXPERT_TILE_ROWS
    up_a = pl.BlockSpec((1, d, f), lambda i, ea, eb, va: (ea[i], 0, 0))
    up_b = pl.BlockSpec((1, d, f), lambda i, ea, eb, va: (eb[i], 0, 0))
    dn_a = pl.BlockSpec((1, f, d), lambda i, ea, eb, va: (ea[i], 0, 0))
    dn_b = pl.BlockSpec((1, f, d), lambda i, ea, eb, va: (eb[i], 0, 0))
    return pl.pallas_call(
        _expert_kernel,
        grid_spec=pltpu.PrefetchScalarGridSpec(
            num_scalar_prefetch=3,
            grid=(n_rows // rt,),
            in_specs=[pl.BlockSpec((rt, xw), lambda i, ea, eb, va: (i, 0)),
                      pl.BlockSpec(nf.shape, lambda i, ea, eb, va: (0, 0)),
                      up_a, up_b, up_a, up_b, dn_a, dn_b],
            out_specs=pl.BlockSpec((rt, d), lambda i, ea, eb, va: (i, 0)),
        ),
        out_shape=jax.ShapeDtypeStruct((n_rows, d), F32),
        compiler_params=_params(("arbitrary",)),
        name="moe_experts",
    )(tile_ea, tile_eb, tile_valid, xs, nf, wg, wg, wu, wu, wd, wd)


def _collect_kernel(dest_ref, ys_ref, fin_ref, o_ref, sem, *, final_norm):
    tm = o_ref.shape[0]
    base = pl.program_id(0) * tm

    def start(r, c):
        _row_copy(ys_ref, dest_ref[base + r], o_ref, r, sem).start()
        return c

    def wait(r, c):
        _row_copy(ys_ref, dest_ref[base + r], o_ref, r, sem).wait()
        return c

    lax.fori_loop(0, tm, start, 0, unroll=ROW_DMA_UNROLL)
    lax.fori_loop(0, tm, wait, 0, unroll=ROW_DMA_UNROLL)
    if final_norm:
        o_ref[...] = _rms(o_ref[...], fin_ref[...])


def _collect(dest, ys, fin, *, final_norm, tm):
    t = dest.shape[0]
    d = ys.shape[1]
    tm = min(tm, t)
    return pl.pallas_call(
        functools.partial(_collect_kernel, final_norm=final_norm),
        grid_spec=pltpu.PrefetchScalarGridSpec(
            num_scalar_prefetch=1,
            grid=(t // tm,),
            in_specs=[pl.BlockSpec(memory_space=pl.ANY),
                      pl.BlockSpec(fin.shape, lambda i, dest: (0, 0))],
            out_specs=pl.BlockSpec((tm, d), lambda i, dest: (i, 0)),
            scratch_shapes=[pltpu.SemaphoreType.DMA],
        ),
        out_shape=jax.ShapeDtypeStruct((t, d), F32),
        compiler_params=_params(("arbitrary",)),
        name="moe_collect",
    )(dest, ys, fin)


def _moe_sparse(x, nf, wr, br, wg, wu, wd, fin, *, n_groups, epg, final_norm):
    t, d = x.shape
    n_pair = epg * (epg - 1) // 2
    n_buckets = n_groups * n_pair
    rt = EXPERT_TILE_ROWS
    n_tiles = t // rt + n_buckets
    xa, cnt = _route(x, nf, wr, br, n_groups=n_groups, epg=epg, tm=512)
    bucket = xa[:, d + 2].astype(jnp.int32)
    rank = xa[:, d + 3].astype(jnp.int32)
    counts = cnt[0, :n_buckets].astype(jnp.int32)
    tiles_per = (counts + rt - 1) // rt
    tile_end = jnp.cumsum(tiles_per)
    dest = ((tile_end - tiles_per) * rt)[bucket] + rank
    tile_id = jnp.arange(n_tiles, dtype=jnp.int32)
    tile_bucket = jnp.minimum(jnp.searchsorted(tile_end, tile_id, side="right"), n_buckets - 1)
    tile_valid = (tile_id < tile_end[-1]).astype(jnp.int32)
    pair_lo = jnp.array([a for a in range(epg) for b in range(a + 1, epg)], jnp.int32)
    pair_hi = jnp.array([b for a in range(epg) for b in range(a + 1, epg)], jnp.int32)
    tile_grp = tile_bucket // n_pair
    tile_ea = (tile_grp * epg + pair_lo[tile_bucket % n_pair]).astype(jnp.int32)
    tile_eb = (tile_grp * epg + pair_hi[tile_bucket % n_pair]).astype(jnp.int32)
    xs = _dispatch(dest, xa, n_tiles * rt, tm=512)
    ys = _experts(tile_ea, tile_eb, tile_valid, xs, nf, wg, wu, wd)
    return _collect(dest, ys, fin, final_norm=final_norm, tm=512)


def _kvq_prompt_kernel(x_ref, nkv_ref, nq_ref, wk_ref, wv_ref, wq_ref,
                       kt_ref, v3_ref, kb_ref, vbt_ref, qbt_ref, *, q_scale, page):
    tm, d = x_ref.shape
    n_heads, dv = v3_ref.shape[1:]
    x = x_ref[...]
    hk = _rms(x, nkv_ref[...]).astype(BF16)
    k = _dot(hk, wk_ref[...])
    v = _dot(hk, wv_ref[...])
    kb_ref[...] = k.astype(BF16)
    for pg in range(tm // page):
        kt_ref[pg * d:(pg + 1) * d, :] = k[pg * page:(pg + 1) * page, :].T
    for h in range(n_heads):
        v3_ref[:, h, :] = v[:, h * dv:(h + 1) * dv]
    hq = _rms(x, nq_ref[...]).astype(BF16)
    q = _dot(hq, wq_ref[...]) * q_scale
    vbt_ref[0] = v.T.astype(BF16)
    qbt_ref[0] = q.T.astype(BF16)


def _kvq_prompt(x, nkv, nq, wk, wv, wq, *, bsz, seq, n_heads, q_scale, page, tm):
    t, d = x.shape
    dv = d // n_heads
    tm = min(tm, seq)
    nt = seq // tm
    row = pl.BlockSpec((tm, d), lambda i: (i, 0))
    tspec = pl.BlockSpec((1, d, tm), lambda i: (i // nt, 0, i % nt))
    kern = functools.partial(_kvq_prompt_kernel, q_scale=q_scale, page=page)
    return pl.pallas_call(
        kern,
        grid=(t // tm,),
        in_specs=[row, _const_spec(nkv.shape), _const_spec(nq.shape),
                  _const_spec(wk.shape), _const_spec(wv.shape), _const_spec(wq.shape)],
        out_specs=[pl.BlockSpec((tm // page * d, page), lambda i: (i, 0)),
                   pl.BlockSpec((tm, n_heads, dv), lambda i: (i, 0, 0)),
                   row, tspec, tspec],
        out_shape=[jax.ShapeDtypeStruct((t // page * d, page), F32),
                   jax.ShapeDtypeStruct((t, n_heads, dv), F32),
                   jax.ShapeDtypeStruct((t, d), BF16),
                   jax.ShapeDtypeStruct((bsz, d, seq), BF16),
                   jax.ShapeDtypeStruct((bsz, d, seq), BF16)],
        compiler_params=_params(("arbitrary",)),
        name="kvq_prompt",
    )(x, nkv, nq, wk, wv, wq)


def _kvq_sample_kernel(x_ref, nkv_ref, nq_ref, wk_ref, wv_ref, wq_ref,
                       kt_ref, k_ref, v3_ref, q_ref, *, q_scale):
    n_heads, dv = v3_ref.shape[1:]
    x = x_ref[...]
    hk = _rms(x, nkv_ref[...]).astype(BF16)
    k = _dot(hk, wk_ref[...])
    v = _dot(hk, wv_ref[...])
    k_ref[...] = k
    kt_ref[...] = k.T
    for h in range(n_heads):
        v3_ref[:, h, :] = v[:, h * dv:(h + 1) * dv]
    hq = _rms(x, nq_ref[...]).astype(BF16)
    q_ref[...] = _dot(hq, wq_ref[...]) * q_scale


def _kvq_sample(x, nkv, nq, wk, wv, wq, *, n_heads, q_scale):
    t, d = x.shape
    dv = d // n_heads
    kern = functools.partial(_kvq_sample_kernel, q_scale=q_scale)
    return pl.pallas_call(
        kern,
        grid=(1,),
        in_specs=[_const_spec(x.shape), _const_spec(nkv.shape), _const_spec(nq.shape),
                  _const_spec(wk.shape), _const_spec(wv.shape), _const_spec(wq.shape)],
        out_specs=[_const_spec((d, t)), _const_spec((t, d)), _const_spec((t, n_heads, dv)),
                   _const_spec((t, d))],
        out_shape=[jax.ShapeDtypeStruct((d, t), F32), jax.ShapeDtypeStruct((t, d), F32),
                   jax.ShapeDtypeStruct((t, n_heads, dv), F32), jax.ShapeDtypeStruct((t, d), F32)],
        compiler_params=_params(("arbitrary",)),
        name="kvq_sample",
    )(x, nkv, nq, wk, wv, wq)


def _lam_value(lq_ref, lk_ref, lam_init):
    lq = lq_ref[...]
    lk = lk_ref[...]
    s = jnp.sum(lq * lk, axis=-1, keepdims=True)
    return jnp.exp(s[0:1]) - jnp.exp(s[1:2]) + lam_init


def _attn_prompt_kernel(slopes_ref, qt_ref, k_ref, kpos_ref, vt_ref, lq_ref, lk_ref, sub_ref, o_ref,
                        *, head_dim, lam_init, tq):
    h = pl.program_id(1)
    dv, seq = qt_ref.shape[1:]
    nf = kpos_ref.shape[1]
    slope = slopes_ref[h]
    lam = _lam_value(lq_ref, lk_ref, lam_init)
    row = lax.broadcasted_iota(jnp.int32, (dv, tq), 0)
    fr = lax.broadcasted_iota(jnp.int32, (SUBLANES, 2 * tq), 0)
    ci = lax.broadcasted_iota(jnp.int32, (SUBLANES, 2 * tq), 1)
    qcol = jnp.where(ci >= tq, ci - tq, ci)
    kr = lax.broadcasted_iota(jnp.int32, (tq, 2 * tq), 0)
    kc = lax.broadcasted_iota(jnp.int32, (tq, 2 * tq), 1)
    visible = jnp.where(kc >= tq, kc - tq, kc) >= kr

    for i in range(seq // tq):
        kv = (i + 1) * tq
        qt = qt_ref[0, :, i * tq:(i + 1) * tq].astype(F32)
        qq = jnp.concatenate([jnp.where(row < head_dim, qt, 0.0),
                              jnp.where(row >= head_dim, qt, 0.0)], axis=1)
        ipos = i * tq + qcol
        ihi = ((ipos // POS_SPLIT) * POS_SPLIT).astype(F32)
        ilo = (ipos % POS_SPLIT).astype(F32)
        feat = jnp.where(fr == 0, -slope * ihi,
                         jnp.where(fr == 1, -slope * ilo,
                                   jnp.where((fr == 2) | (fr == 3), slope, 0.0)))
        qqa = jnp.concatenate([qq, feat, jnp.zeros((nf - SUBLANES, 2 * tq), F32)],
                              axis=0).astype(BF16)
        ka = jnp.concatenate([k_ref[:kv, :], kpos_ref[:kv, :]], axis=1)
        s = _dot(ka, qqa)
        diag = jnp.where(visible, s[kv - tq:, :], MASK_VALUE)
        s = jnp.concatenate([s[:kv - tq, :], diag], axis=0) if i > 0 else diag
        m = jnp.max(s, axis=0, keepdims=True)
        p = jnp.exp(s - m)
        l = jnp.sum(p, axis=0, keepdims=True)
        o2 = _dot(vt_ref[0, :, :kv], p.astype(BF16)) / l
        ot = o2[:, :tq] - lam * o2[:, tq:]
        ms = jnp.mean(ot * ot, axis=0, keepdims=True)
        ot = ot * lax.rsqrt(ms + RMS_EPS)
        o_ref[i * tq:(i + 1) * tq, :] = (ot.T * sub_ref[...] * (1.0 - lam_init)).astype(o_ref.dtype)


def _attn_prompt(slopes, qbt, kb, kpos, vbt, lq, lk, sub, *, bsz, seq, n_heads, head_dim, lam_init, tq):
    _, d, _ = qbt.shape
    dv = d // n_heads
    kern = functools.partial(_attn_prompt_kernel, head_dim=head_dim, lam_init=lam_init, tq=tq)
    tspec = pl.BlockSpec((1, dv, seq), lambda b, h: (b, h, 0))
    return pl.pallas_call(
        kern,
        grid=(bsz, n_heads),
        in_specs=[pl.BlockSpec(memory_space=pltpu.SMEM),
                  tspec,
                  pl.BlockSpec((seq, dv), lambda b, h: (b, h)),
                  pl.BlockSpec(kpos.shape, lambda b, h: (0, 0)),
                  tspec,
                  pl.BlockSpec(lq.shape, lambda b, h: (0, 0)),
                  pl.BlockSpec(lk.shape, lambda b, h: (0, 0)),
                  pl.BlockSpec(sub.shape, lambda b, h: (0, 0))],
        out_specs=pl.BlockSpec((seq, dv), lambda b, h: (b, h)),
        out_shape=jax.ShapeDtypeStruct((bsz * seq, d), BF16),
        compiler_params=_params(("arbitrary", "arbitrary")),
        name="attn_prompt",
    )(slopes, qbt, kb, kpos, vbt, lq, lk, sub)


def _lane_spread(row, n):
    return jnp.broadcast_to(row, (n, row.shape[1])).T


def _sample_scores_kernel(pt_ref, q_ref, ks_ref, slope_ref, *refs, pps, n_maps, past_len):
    k_refs = refs[:pps]
    s_ref, sself_ref, qb_ref = refs[pps:]
    j = pl.program_id(1)
    d, page = qb_ref.shape
    n_heads = n_maps // 2
    hd = d // n_maps

    def map_sums(t):
        t3 = t.reshape(n_heads, 2 * hd, page)
        return jnp.concatenate([jnp.sum(t3[:, :hd, :], axis=1), jnp.sum(t3[:, hd:, :], axis=1)],
                               axis=0)

    @pl.when(j == 0)
    def _():
        qb = _lane_spread(q_ref[0], page)
        qb_ref[...] = qb
        sself_ref[0] = map_sums(qb * _lane_spread(ks_ref[0], page))

    qb = qb_ref[...]
    slope = slope_ref[...]
    lane = lax.broadcasted_iota(jnp.int32, (n_maps, page), 1)
    for i in range(pps):
        s = map_sums(k_refs[i][0] * qb)
        dist = (past_len - (j * pps + i) * page) - lane
        s_ref[0, :, i * page:(i + 1) * page] = s - slope * dist.astype(F32)


def _sample_scores(pt, q3, ks3, slope_rows, cache_kt, *, n_pages, pps):
    bd, _, d = q3.shape
    n_pool, _, page = cache_kt.shape
    n_maps = slope_rows.shape[0]
    past_len = n_pages * page
    tok = pl.BlockSpec((1, 1, d), lambda b, j, pt: (b, 0, 0))

    def page_spec(i):
        return pl.BlockSpec((1, d, page), lambda b, j, pt: (pt[b * n_pages + j * pps + i], 0, 0))

    kern = functools.partial(_sample_scores_kernel, pps=pps, n_maps=n_maps, past_len=past_len)
    return pl.pallas_call(
        kern,
        grid_spec=pltpu.PrefetchScalarGridSpec(
            num_scalar_prefetch=1,
            grid=(bd, n_pages // pps),
            in_specs=[tok, tok, pl.BlockSpec(slope_rows.shape, lambda b, j, pt: (0, 0))]
                     + [page_spec(i) for i in range(pps)],
            out_specs=[pl.BlockSpec((1, n_maps, pps * page), lambda b, j, pt: (b, 0, j)),
                       pl.BlockSpec((1, n_maps, page), lambda b, j, pt: (b, 0, 0))],
            scratch_shapes=[pltpu.VMEM((d, page), F32)],
        ),
        out_shape=[jax.ShapeDtypeStruct((bd, n_maps, past_len), F32),
                   jax.ShapeDtypeStruct((bd, n_maps, page), F32)],
        compiler_params=_params(("arbitrary", "arbitrary")),
        name="sample_scores",
    )(pt, q3, ks3, slope_rows, *([cache_kt] * pps))


def _sample_probs_kernel(s_ref, sself_ref, lq_ref, lk_ref, a_ref, aself_ref, *, n_heads, lam_init):
    lam = _lam_value(lq_ref, lk_ref, lam_init)

    def normalized(mp):
        s = s_ref[:, mp * n_heads:(mp + 1) * n_heads, :]
        ss = sself_ref[:, mp * n_heads:(mp + 1) * n_heads, :]
        m = jnp.maximum(jnp.max(s, axis=-1, keepdims=True), ss[:, :, :1])
        p = jnp.exp(s - m)
        ps = jnp.exp(ss - m)
        l = jnp.sum(p, axis=-1, keepdims=True) + ps[:, :, :1]
        return p / l, ps / l

    p0, ps0 = normalized(0)
    p1, ps1 = normalized(1)
    a_ref[...] = p0 - lam * p1
    aself_ref[...] = ps0 - lam * ps1


def _sample_probs(s, sself, lq, lk, *, n_heads, lam_init, sb):
    bd, n_maps, past = s.shape
    lanes = sself.shape[2]
    sb = math.gcd(sb, bd)
    kern = functools.partial(_sample_probs_kernel, n_heads=n_heads, lam_init=lam_init)
    return pl.pallas_call(
        kern,
        grid=(bd // sb,),
        in_specs=[pl.BlockSpec((sb, n_maps, past), lambda i: (i, 0, 0)),
                  pl.BlockSpec((sb, n_maps, lanes), lambda i: (i, 0, 0)),
                  _const_spec(lq.shape), _const_spec(lk.shape)],
        out_specs=[pl.BlockSpec((sb, n_heads, past), lambda i: (i, 0, 0)),
                   pl.BlockSpec((sb, n_heads, lanes), lambda i: (i, 0, 0))],
        out_shape=[jax.ShapeDtypeStruct((bd, n_heads, past), F32),
                   jax.ShapeDtypeStruct((bd, n_heads, lanes), F32)],
        compiler_params=_params(("arbitrary",)),
        name="sample_probs",
    )(s, sself, lq, lk)


def _sample_values_kernel(pt_ref, at_ref, aself_ref, vs_ref, sub_ref, *refs, pps, lam_init):
    v_refs = refs[:pps]
    o_ref, acc_ref = refs[pps:]
    j = pl.program_id(1)
    nj = pl.num_programs(1)
    n_heads, dv = acc_ref.shape
    rows, lanes = v_refs[0].shape[1:]
    pos_per_blk = lanes // n_heads
    blks = rows // lanes

    @pl.when(j == 0)
    def _():
        acc_ref[...] = jnp.zeros(acc_ref.shape, F32)

    acc = acc_ref[...]
    for i in range(pps):
        for blk in range(blks):
            r = at_ref[0, pl.ds((j * pps + i) * blks + blk, 1), :]
            w = _lane_spread(r, dv)
            prod = w * v_refs[i][0, blk * lanes:(blk + 1) * lanes, :]
            acc = acc + jnp.sum(prod.reshape(pos_per_blk, n_heads, dv), axis=0)
    acc_ref[...] = acc

    @pl.when(j == nj - 1)
    def _():
        o = acc + aself_ref[0] * vs_ref[0]
        ms = jnp.mean(o * o, axis=-1, keepdims=True)
        o_ref[0] = (o * lax.rsqrt(ms + RMS_EPS) * sub_ref[...] * (1.0 - lam_init)).astype(o_ref.dtype)


def _sample_values(pt, at, aself, vs3, sub, cache_v2, *, n_pages, pps, lam_init):
    bd, n_heads, dv = vs3.shape
    n_pool, rows, lanes = cache_v2.shape
    assert lanes == dv and aself.shape[2] == dv

    def page_spec(i):
        return pl.BlockSpec((1, rows, lanes), lambda b, j, pt: (pt[b * n_pages + j * pps + i], 0, 0))

    per = lambda shape: pl.BlockSpec((1,) + tuple(shape[1:]), lambda b, j, pt: (b, 0, 0))
    kern = functools.partial(_sample_values_kernel, pps=pps, lam_init=lam_init)
    return pl.pallas_call(
        kern,
        grid_spec=pltpu.PrefetchScalarGridSpec(
            num_scalar_prefetch=1,
            grid=(bd, n_pages // pps),
            in_specs=[per(at.shape), per(aself.shape), per(vs3.shape),
                      pl.BlockSpec(sub.shape, lambda b, j, pt: (0, 0))]
                     + [page_spec(i) for i in range(pps)],
            out_specs=per((bd, n_heads, dv)),
            scratch_shapes=[pltpu.VMEM((n_heads, dv), F32)],
        ),
        out_shape=jax.ShapeDtypeStruct((bd, n_heads, dv), BF16),
        compiler_params=_params(("arbitrary", "arbitrary")),
        name="sample_values",
    )(pt, at, aself, vs3, sub, *([cache_v2] * pps))


def _oproj_kernel(x_ref, a_ref, w_ref, o_ref):
    o_ref[...] = x_ref[...] + _dot(a_ref[...], w_ref[...])


def _oproj(x, a, w, *, tm):
    t, d = x.shape
    tm = min(tm, t)
    row = pl.BlockSpec((tm, d), lambda i: (i, 0))
    return pl.pallas_call(
        _oproj_kernel,
        grid=(t // tm,),
        in_specs=[row, pl.BlockSpec((tm, a.shape[1]), lambda i: (i, 0)), _const_spec(w.shape)],
        out_specs=row,
        out_shape=jax.ShapeDtypeStruct((t, d), F32),
        compiler_params=_params(("arbitrary",)),
        name="attn_oproj",
    )(x, a, w)


def kernel(x_prompt, x_sample, cache_k, cache_v, page_table, norm_mix, norm_ffn, w_uv, sgu_gain, w_sgu, b_sgu, w_gmlp_o, norm_kv, w_k, w_v, w_q, lam_q, lam_k, subln_gain, w_attn_o, w_router_group, b_router_group, w_router_expert, b_router_expert, w_e_gate, w_e_up, w_e_down, norm_final):
    bsz, seq, d = x_prompt.shape
    dec_b, dec_s, _ = x_sample.shape
    assert dec_s == 1, "sample group handles one new token per sequence"
    n_pool, page, n_heads, n_qk_maps, head_dim = cache_k.shape
    v_dim = cache_v.shape[-1]
    assert n_heads * v_dim == d and n_qk_maps * head_dim == v_dim and n_qk_maps == 2
    assert page == LANES and v_dim == LANES and n_heads == SUBLANES
    n_a = w_uv.shape[0]
    depth = norm_mix.shape[0]
    assert n_a == 1 and depth == 2, "one gMLP layer followed by one attention layer"
    _, n_sgu, chunk, _ = w_sgu.shape
    d_gm = sgu_gain.shape[1]
    sgw = d_gm // n_sgu
    n_groups = w_router_group.shape[2]
    epg = w_router_expert.shape[3]
    n_e = n_groups * epg
    n_pages = page_table.shape[1]
    past_len = n_pages * page
    assert seq % chunk == 0 and seq % page == 0 and seq <= POS_SPLIT * POS_SPLIT

    row2 = lambda a: a.reshape(1, -1)
    wuv = w_uv[0].astype(BF16)
    wo_g = w_gmlp_o[0].astype(BF16)
    wk = w_k.astype(BF16)
    wv = w_v.astype(BF16)
    wq = w_q[0].astype(BF16)
    wao = w_attn_o[0].astype(BF16)
    wg = w_e_gate.astype(BF16)
    wu = w_e_up.astype(BF16)
    wd = w_e_down.astype(BF16)
    wr = jnp.concatenate([jnp.transpose(w_router_expert, (0, 2, 1, 3)).reshape(depth, d, n_e),
                          w_router_group], axis=2)
    br = jnp.concatenate([b_router_expert.reshape(depth, 1, n_e),
                          b_router_group.reshape(depth, 1, n_groups)], axis=2)
    fin = row2(norm_final)
    slopes = 2.0 ** (-8.0 * jnp.arange(1, n_heads + 1, dtype=F32) / n_heads)
    lam_init = 0.8 - 0.6 * math.exp(-0.3 * 1)
    q_scale = head_dim ** -0.5
    ws_row = jnp.repeat(w_sgu[0, :, 0, 0], sgw).reshape(1, d_gm)
    bs_row = jnp.repeat(b_sgu[0, :, 0], sgw).reshape(1, d_gm)
    bs_t = b_sgu[0].T
    sub = row2(subln_gain[0])
    kp = jnp.arange(seq, dtype=jnp.int32)
    kpos = jnp.zeros((seq, LANES), F32)
    kpos = kpos.at[:, 0].set(1.0).at[:, 1].set(1.0)
    kpos = kpos.at[:, 2].set(((kp // POS_SPLIT) * POS_SPLIT).astype(F32))
    kpos = kpos.at[:, 3].set((kp % POS_SPLIT).astype(F32)).astype(BF16)

    def moe(x, i, final_norm, tm):
        return _moe_dense(x, row2(norm_ffn[i]), wr[i], br[i],
                          wg[i], wu[i], wd[i], fin, n_groups=n_groups, epg=epg,
                          final_norm=final_norm, tm=tm)

    def moe_prompt(x, i, final_norm):
        return _moe_sparse(x, row2(norm_ffn[i]), wr[i], br[i],
                           wg[i], wu[i], wd[i], fin, n_groups=n_groups, epg=epg,
                           final_norm=final_norm)

    tq = min(256, seq)
    xp = x_prompt.reshape(bsz * seq, d)
    (xp,) = _gmlp(xp, row2(norm_mix[0]), wuv, row2(sgu_gain[0]), w_sgu[0], bs_t, wo_g,
                  seq_one=False, chunk=chunk, n_groups=n_sgu, tm=256)
    xp = moe_prompt(xp, 0, False)
    kt_p, v3_p, kb, vbt, qbt = _kvq_prompt(xp, row2(norm_kv), row2(norm_mix[1]), wk, wv, wq,
                                           bsz=bsz, seq=seq, n_heads=n_heads, q_scale=q_scale,
                                           page=page, tm=512)
    att = _attn_prompt(slopes, qbt, kb, kpos, vbt, lam_q[0], lam_k[0], sub,
                       bsz=bsz, seq=seq, n_heads=n_heads, head_dim=head_dim, lam_init=lam_init,
                       tq=tq)
    xp = _oproj(xp, att, wao, tm=512)
    y_prompt = moe_prompt(xp, 1, True).reshape(bsz, seq, d)

    xs = x_sample.reshape(dec_b, d)
    xs, gv = _gmlp(xs, row2(norm_mix[0]), wuv, row2(sgu_gain[0]), ws_row, bs_row, wo_g,
                   seq_one=True, chunk=chunk, n_groups=n_sgu, tm=128)
    xs = moe(xs, 0, False, 128)
    kt_s, k_s, v3_s, q_s = _kvq_sample(xs, row2(norm_kv), row2(norm_mix[1]), wk, wv, wq,
                                       n_heads=n_heads, q_scale=q_scale)
    cache_kt = jnp.transpose(cache_k, (0, 2, 3, 4, 1)).reshape(n_pool, d, page)
    cache_v2 = cache_v.reshape(n_pool, page * n_heads, v_dim)
    pt = page_table.reshape(-1)
    pps = math.gcd(8, n_pages)
    slope_rows = jnp.tile(slopes, n_qk_maps).reshape(n_heads * n_qk_maps, 1)
    s, sself = _sample_scores(pt, q_s.reshape(dec_b, 1, d), k_s.reshape(dec_b, 1, d), slope_rows,
                              cache_kt, n_pages=n_pages, pps=pps)
    a, aself = _sample_probs(s, sself, lam_q[0], lam_k[0], n_heads=n_heads, lam_init=lam_init, sb=8)
    at = jnp.transpose(a, (0, 2, 1)).reshape(dec_b, past_len * n_heads // LANES, LANES)
    att_s = _sample_values(pt, at, aself, v3_s, sub, cache_v2, n_pages=n_pages, pps=pps,
                           lam_init=lam_init)
    xs = _oproj(xs, att_s.reshape(dec_b, d), wao, tm=128)
    y_sample = moe(xs, 1, True, 128).reshape(dec_b, 1, d)

    n_pp = seq // page
    k_prompt = jnp.transpose(kt_p.reshape(bsz, n_pp, n_heads, n_qk_maps, head_dim, page),
                             (0, 1, 5, 2, 3, 4))
    v_prompt = v3_p.reshape(bsz, n_pp, page, n_heads, v_dim)
    k_sample = jnp.transpose(kt_s.reshape(n_heads, n_qk_maps, head_dim, dec_b),
                             (3, 0, 1, 2)).reshape(dec_b, 1, n_heads, n_qk_maps, head_dim)
    v_sample = v3_s.reshape(dec_b, 1, n_heads, v_dim)
    gmlp_v_sample = gv.reshape(n_a, dec_b, 1, d_gm)
    return (y_prompt, y_sample, k_prompt, v_prompt, k_sample, v_sample, gmlp_v_sample)
```

```python
import functools
import math

import jax
import jax.numpy as jnp
from jax import lax
from jax.experimental import pallas as pl
from jax.experimental.pallas import tpu as pltpu

F32 = jnp.float32
BF16 = jnp.bfloat16

RMS_EPS = 1e-6
MASK_VALUE = -1e30
TOP_K_INNER = 2
LANES = 128
SUBLANES = 8
VMEM_LIMIT_BYTES = 56 * 1024 * 1024
POS_SPLIT = 256


def _rms(x, g):
    ms = jnp.mean(x * x, axis=-1, keepdims=True)
    return x * lax.rsqrt(ms + RMS_EPS) * g


def _gelu_tanh(x):
    c = math.sqrt(2.0 / math.pi)
    return x * (0.5 * (1.0 + jnp.tanh(c * (x + 0.044715 * (x * x * x)))))


def _dot(a, b):
    return jnp.dot(a, b, preferred_element_type=F32)


def _params(sem):
    return pltpu.CompilerParams(dimension_semantics=sem, vmem_limit_bytes=VMEM_LIMIT_BYTES)


def _const_spec(shape):
    nd = len(shape)
    return pl.BlockSpec(shape, lambda *_: (0,) * nd)


def _gmlp_kernel(x_ref, nm_ref, wuv_ref, sg_ref, ws_ref, bs_ref, wo_ref, *out_refs,
                 seq_one, chunk, n_groups, slab_groups):
    if seq_one:
        o_ref, v_ref = out_refs
    else:
        (o_ref,) = out_refs
    tm, _ = x_ref.shape
    d_gm = sg_ref.shape[1]
    sgw = d_gm // n_groups
    x = x_ref[...]
    h = _rms(x, nm_ref[...]).astype(BF16)
    v = _gelu_tanh(_dot(h, wuv_ref[:, d_gm:]))
    vn = _rms(v, sg_ref[...])
    if seq_one:
        v_ref[...] = vn
        vb = vn.astype(BF16).astype(F32)
    else:
        vb = vn.astype(BF16)
        row = lax.broadcasted_iota(jnp.int32, (chunk, chunk), 0)
        col = lax.broadcasted_iota(jnp.int32, (chunk, chunk), 1)
        causal = row >= col
    slab = slab_groups * sgw
    acc = x
    for s in range(n_groups // slab_groups):
        c0 = s * slab
        u = _gelu_tanh(_dot(h, wuv_ref[:, c0:c0 + slab]))
        if seq_one:
            gate = vb[:, c0:c0 + slab] * ws_ref[:, c0:c0 + slab] + bs_ref[:, c0:c0 + slab]
        else:
            cols = []
            for gg in range(slab_groups):
                g = s * slab_groups + gg
                w = jnp.where(causal, ws_ref[g], 0.0).astype(BF16)
                bias = jnp.broadcast_to(bs_ref[:, g:g + 1], (chunk, sgw))
                rows = []
                for c in range(tm // chunk):
                    vv = vb[c * chunk:(c + 1) * chunk, g * sgw:(g + 1) * sgw]
                    rows.append(_dot(w, vv) + bias)
                cols.append(jnp.concatenate(rows, axis=0) if len(rows) > 1 else rows[0])
            gate = jnp.concatenate(cols, axis=1) if len(cols) > 1 else cols[0]
        p = (u * gate).astype(BF16)
        acc = acc + _dot(p, wo_ref[c0:c0 + slab, :])
    o_ref[...] = acc


def _gmlp(x, nm, wuv, sg, ws, bs, wo, *, seq_one, chunk, n_groups, tm):
    t, d = x.shape
    d_gm = sg.shape[1]
    tm = min(tm, t)
    slab_groups = 2 if n_groups % 2 == 0 else 1
    out_shape = [jax.ShapeDtypeStruct((t, d), F32)]
    out_specs = [pl.BlockSpec((tm, d), lambda i: (i, 0))]
    if seq_one:
        out_shape.append(jax.ShapeDtypeStruct((t, d_gm), F32))
        out_specs.append(pl.BlockSpec((tm, d_gm), lambda i: (i, 0)))
    kern = functools.partial(_gmlp_kernel, seq_one=seq_one, chunk=chunk, n_groups=n_groups,
                             slab_groups=slab_groups)
    return pl.pallas_call(
        kern,
        grid=(t // tm,),
        in_specs=[pl.BlockSpec((tm, d), lambda i: (i, 0)),
                  _const_spec(nm.shape), _const_spec(wuv.shape), _const_spec(sg.shape),
                  _const_spec(ws.shape), _const_spec(bs.shape), _const_spec(wo.shape)],
        out_specs=out_specs,
        out_shape=out_shape,
        compiler_params=_params(("arbitrary",)),
        name="gmlp_seq1" if seq_one else "gmlp",
    )(x, nm, wuv, sg, ws, bs, wo)


def _dot_split(a, b):
    a_hi = a.astype(BF16)
    a_lo = (a - a_hi.astype(F32)).astype(BF16)
    b_hi = b.astype(BF16)
    b_lo = (b - b_hi.astype(F32)).astype(BF16)
    return _dot(a_hi, b_hi) + _dot(a_lo, b_hi) + _dot(a_hi, b_lo)


def _router_top2(hf, wr, br, n_groups, epg):
    n_e = n_groups * epg
    logits = _dot_split(hf, wr) + br
    lane = lax.broadcasted_iota(jnp.int32, logits.shape, 1)
    neg = jnp.float32(-jnp.inf)
    is_grp = lane >= n_e
    mg = jnp.max(jnp.where(is_grp, logits, neg), axis=-1, keepdims=True)
    p_grp = 1.0 / jnp.sum(jnp.where(is_grp, jnp.exp(logits - mg), 0.0), axis=-1, keepdims=True)
    grp = jnp.min(jnp.where(is_grp & (logits == mg), lane - n_e, n_groups), axis=-1, keepdims=True)
    in_grp = (lane >= grp * epg) & (lane < grp * epg + epg)
    el1 = jnp.where(in_grp, logits, neg)
    v1 = jnp.max(el1, axis=-1, keepdims=True)
    i1 = jnp.min(jnp.where(el1 == v1, lane, n_e), axis=-1, keepdims=True)
    el2 = jnp.where(lane == i1, neg, el1)
    v2 = jnp.max(el2, axis=-1, keepdims=True)
    i2 = jnp.min(jnp.where(el2 == v2, lane, n_e), axis=-1, keepdims=True)
    e2 = jnp.exp(v2 - v1)
    w1 = 1.0 / (1.0 + e2)
    w2 = e2 / (1.0 + e2)
    return grp, i1, i2, w1 * p_grp, w2 * p_grp


def _router_gate(hf, wr, br, n_groups, epg):
    _, i1, i2, g1, g2 = _router_top2(hf, wr, br, n_groups, epg)
    lane_e = lax.broadcasted_iota(jnp.int32, (hf.shape[0], n_groups * epg), 1)
    return jnp.where(lane_e == i1, g1, 0.0) + jnp.where(lane_e == i2, g2, 0.0)


def _moe_dense_kernel(x_ref, nf_ref, wr_ref, br_ref, wg_ref, wu_ref, wd_ref,
                      fin_ref, o_ref, hb_ref, gate_ref, *, n_groups, epg, final_norm):
    e = pl.program_id(1)
    n_e = n_groups * epg

    @pl.when(e == 0)
    def _():
        x = x_ref[...]
        hf = _rms(x, nf_ref[...])
        hb_ref[...] = hf.astype(BF16)
        gate_ref[...] = _router_gate(hf, wr_ref[...], br_ref[...], n_groups, epg)
        o_ref[...] = x

    hb = hb_ref[...]
    gate = gate_ref[...]
    lane_e = lax.broadcasted_iota(jnp.int32, gate.shape, 1)
    ge = jnp.sum(jnp.where(lane_e == e, gate, 0.0), axis=-1, keepdims=True)
    a = _dot(hb, wg_ref[0])
    b = _dot(hb, wu_ref[0])
    hid = (a * jax.nn.sigmoid(a)) * b
    o_ref[...] += _dot((hid * ge).astype(BF16), wd_ref[0])

    if final_norm:
        @pl.when(e == n_e - 1)
        def _():
            o_ref[...] = _rms(o_ref[...], fin_ref[...])


def _moe_dense(x, nf, wr, br, wg, wu, wd, fin, *, n_groups, epg, final_norm, tm):
    t, d = x.shape
    n_e, _, f = wg.shape
    tm = min(tm, t)
    kern = functools.partial(_moe_dense_kernel, n_groups=n_groups, epg=epg, final_norm=final_norm)
    return pl.pallas_call(
        kern,
        grid=(t // tm, n_e),
        in_specs=[pl.BlockSpec((tm, d), lambda i, e: (i, 0)),
                  _const_spec(nf.shape), _const_spec(wr.shape), _const_spec(br.shape),
                  pl.BlockSpec((1, d, f), lambda i, e: (e, 0, 0)),
                  pl.BlockSpec((1, d, f), lambda i, e: (e, 0, 0)),
                  pl.BlockSpec((1, f, d), lambda i, e: (e, 0, 0)),
                  _const_spec(fin.shape)],
        out_specs=pl.BlockSpec((tm, d), lambda i, e: (i, 0)),
        out_shape=jax.ShapeDtypeStruct((t, d), F32),
        scratch_shapes=[pltpu.VMEM((tm, d), BF16), pltpu.VMEM((tm, n_e), F32)],
        compiler_params=_params(("arbitrary", "arbitrary")),
        name="moe_dense",
    )(x, nf, wr, br, wg, wu, wd, fin)


EXPERT_TILE_ROWS = 256
ROUTE_ROWS = 32
INFO_ROWS = 8


def _dot_nt(a, b):
    return lax.dot_general(a, b, (((1,), (1,)), ((), ())), preferred_element_type=F32)


def _route_kernel(x_ref, nf_ref, wrt_ref, brt_ref, before_ref, xa_ref, info_ref, cnt_ref, carry_ref,
                  *, n_groups, epg):
    i = pl.program_id(0)
    tm, d = x_ref.shape
    lanes = cnt_ref.shape[1]
    n_e = n_groups * epg
    n_pair = epg * (epg - 1) // 2
    rr = wrt_ref.shape[0]

    @pl.when(i == 0)
    def _():
        carry_ref[...] = jnp.zeros(carry_ref.shape, F32)

    x = x_ref[...]
    hf = _rms(x, nf_ref[...])
    h_hi = hf.astype(BF16)
    h_lo = (hf - h_hi.astype(F32)).astype(BF16)
    w = wrt_ref[...]
    w_hi = w.astype(BF16)
    w_lo = (w - w_hi.astype(F32)).astype(BF16)
    both = _dot_nt(jnp.concatenate([w_hi, w_lo], axis=0), h_hi)
    logits = both[:rr] + both[rr:] + _dot_nt(w_hi, h_lo) + brt_ref[:, :1]
    row = lax.broadcasted_iota(jnp.int32, (rr, tm), 0)
    neg = jnp.float32(-jnp.inf)
    is_grp = (row >= n_e) & (row < n_e + n_groups)
    mg = jnp.max(jnp.where(is_grp, logits, neg), axis=0, keepdims=True)
    p_grp = 1.0 / jnp.sum(jnp.where(is_grp, jnp.exp(logits - mg), 0.0), axis=0, keepdims=True)
    grp = jnp.min(jnp.where(is_grp & (logits == mg), row - n_e, n_groups), axis=0, keepdims=True)
    in_grp = (row >= grp * epg) & (row < grp * epg + epg)
    el1 = jnp.where(in_grp, logits, neg)
    v1 = jnp.max(el1, axis=0, keepdims=True)
    i1 = jnp.min(jnp.where(el1 == v1, row, n_e), axis=0, keepdims=True)
    el2 = jnp.where(row == i1, neg, el1)
    v2 = jnp.max(el2, axis=0, keepdims=True)
    i2 = jnp.min(jnp.where(el2 == v2, row, n_e), axis=0, keepdims=True)
    e2 = jnp.exp(v2 - v1)
    g1 = p_grp / (1.0 + e2)
    g2 = p_grp * (e2 / (1.0 + e2))
    first = i1 < i2
    la = jnp.where(first, i1, i2) - grp * epg
    lb = jnp.where(first, i2, i1) - grp * epg
    ga = jnp.where(first, g1, g2)
    gb = jnp.where(first, g2, g1)
    pair = (la * (2 * epg - 1 - la)) // 2 + (lb - la - 1)
    bucket = grp * n_pair + pair
    onehot = jnp.where(row == bucket, 1.0, 0.0)
    prefix = _dot(onehot.astype(BF16), before_ref[...])
    carry = carry_ref[...]
    rank = jnp.sum(onehot * (prefix + carry[:, :1]), axis=0, keepdims=True)
    carry = carry + jnp.sum(onehot, axis=1, keepdims=True)
    carry_ref[...] = carry
    cnt_ref[...] = carry
    ir = lax.broadcasted_iota(jnp.int32, (lanes, tm), 0)
    info = jnp.where(ir == 0, ga, jnp.where(ir == 1, gb,
                     jnp.where(ir == 2, bucket.astype(F32), jnp.where(ir == 3, rank, 0.0))))
    info_ref[0] = info[:INFO_ROWS]
    xa_ref[:, :d] = x
    xa_ref[:, d:] = info.T


def _route(x, nf, wrt, brt, *, n_groups, epg, tm):
    t, d = x.shape
    tm = min(tm, t)
    rr = wrt.shape[0]
    before = (jnp.arange(tm)[:, None] < jnp.arange(tm)[None, :]).astype(BF16)
    kern = functools.partial(_route_kernel, n_groups=n_groups, epg=epg)
    return pl.pallas_call(
        kern,
        grid=(t // tm,),
        in_specs=[pl.BlockSpec((tm, d), lambda i: (i, 0)),
                  _const_spec(nf.shape), _const_spec(wrt.shape), _const_spec(brt.shape),
                  _const_spec(before.shape)],
        out_specs=[pl.BlockSpec((tm, d + LANES), lambda i: (i, 0)),
                   pl.BlockSpec((1, INFO_ROWS, tm), lambda i: (i, 0, 0)),
                   _const_spec((rr, LANES))],
        out_shape=[jax.ShapeDtypeStruct((t, d + LANES), F32),
                   jax.ShapeDtypeStruct((t // tm, INFO_ROWS, tm), F32),
                   jax.ShapeDtypeStruct((rr, LANES), F32)],
        scratch_shapes=[pltpu.VMEM((rr, LANES), F32)],
        compiler_params=_params(("arbitrary",)),
        name="moe_route",
    )(x, nf, wrt, brt, before)


def _gather_rows(src_ref, idx_ref, base, dst_buf, sem, n_rows, wait):
    for r in range(n_rows):
        cp = pltpu.make_async_copy(src_ref.at[pl.ds(idx_ref[base + r], 1), :],
                                   dst_buf.at[pl.ds(r, 1), :], sem)
        if wait:
            cp.wait()
        else:
            cp.start(priority=r % 2)


def _expert_kernel(tok_ref, ea_ref, eb_ref, nv_ref, xa_ref, nf_ref, fin_ref, wga_ref, wgb_ref,
                   wua_ref, wub_ref, wda_ref, wdb_ref, ys_ref, xbuf, sem, *, final_norm):
    i = pl.program_id(0)
    n_valid = nv_ref[0]
    rt, d = ys_ref.shape
    slot = i % 2

    @pl.when((i == 0) & (n_valid > 0))
    def _():
        _gather_rows(xa_ref, tok_ref, 0, xbuf.at[0], sem.at[0], rt, False)

    @pl.when(i < n_valid)
    def _():
        _gather_rows(xa_ref, tok_ref, i * rt, xbuf.at[slot], sem.at[slot], rt, True)
        _gather_rows(xa_ref, tok_ref, (i + 1) * rt, xbuf.at[1 - slot], sem.at[1 - slot], rt, False)
        x = xbuf[slot, :, :d]
        gates = xbuf[slot, :, d:]
        h = _rms(x, nf_ref[...]).astype(BF16)

        def expert(wg_ref, wu_ref, wd_ref, gate):
            a = _dot(h, wg_ref[0])
            b = _dot(h, wu_ref[0])
            hid = (a * jax.nn.sigmoid(a)) * b
            return _dot((hid * gate).astype(BF16), wd_ref[0])

        y = (x + expert(wga_ref, wua_ref, wda_ref, gates[:, 0:1])
             + expert(wgb_ref, wub_ref, wdb_ref, gates[:, 1:2]))
        if final_norm:
            y = _rms(y, fin_ref[...])
        ys_ref[...] = y

    @pl.when((i == n_valid) & (n_valid > 0))
    def _():
        _gather_rows(xa_ref, tok_ref, i * rt, xbuf.at[slot], sem.at[slot], rt, True)


def _experts(tok, tile_ea, tile_eb, n_valid, xa, nf, fin, wg, wu, wd, *, final_norm):
    n_e, d, f = wg.shape
    xw = xa.shape[1]
    rt = EXPERT_TILE_ROWS
    n_tiles = tok.shape[0] // rt
    up_a = pl.BlockSpec((1, d, f), lambda i, tok, ea, eb, nv: (ea[i], 0, 0))
    up_b = pl.BlockSpec((1, d, f), lambda i, tok, ea, eb, nv: (eb[i], 0, 0))
    dn_a = pl.BlockSpec((1, f, d), lambda i, tok, ea, eb, nv: (ea[i], 0, 0))
    dn_b = pl.BlockSpec((1, f, d), lambda i, tok, ea, eb, nv: (eb[i], 0, 0))
    small = lambda a: pl.BlockSpec(a.shape, lambda i, tok, ea, eb, nv: (0, 0))
    kern = functools.partial(_expert_kernel, final_norm=final_norm)
    return pl.pallas_call(
        kern,
        grid_spec=pltpu.PrefetchScalarGridSpec(
            num_scalar_prefetch=4,
            grid=(n_tiles,),
            in_specs=[pl.BlockSpec(memory_space=pl.ANY), small(nf), small(fin),
                      up_a, up_b, up_a, up_b, dn_a, dn_b],
            out_specs=pl.BlockSpec((rt, d), lambda i, tok, ea, eb, nv: (i, 0)),
            scratch_shapes=[pltpu.VMEM((2, rt, xw), F32), pltpu.SemaphoreType.DMA((2,))],
        ),
        out_shape=jax.ShapeDtypeStruct((n_tiles * rt, d), F32),
        compiler_params=_params(("arbitrary",)),
        name="moe_experts",
    )(tok, tile_ea, tile_eb, n_valid, xa, nf, fin, wg, wg, wu, wu, wd, wd)


def _collect_kernel(slot_ref, ys_ref, o_ref, sem):
    tm = o_ref.shape[0]
    base = pl.program_id(0) * tm
    _gather_rows(ys_ref, slot_ref, base, o_ref, sem, tm, False)
    _gather_rows(ys_ref, slot_ref, base, o_ref, sem, tm, True)


def _collect(slot, ys, *, tm):
    t = slot.shape[0]
    d = ys.shape[1]
    tm = min(tm, t)
    return pl.pallas_call(
        _collect_kernel,
        grid_spec=pltpu.PrefetchScalarGridSpec(
            num_scalar_prefetch=1,
            grid=(t // tm,),
            in_specs=[pl.BlockSpec(memory_space=pl.ANY)],
            out_specs=pl.BlockSpec((tm, d), lambda i, slot: (i, 0)),
            scratch_shapes=[pltpu.SemaphoreType.DMA],
        ),
        out_shape=jax.ShapeDtypeStruct((t, d), F32),
        compiler_params=_params(("arbitrary",)),
        name="moe_collect",
    )(slot, ys)


def _moe_sparse(x, nf, wrt, brt, wg, wu, wd, fin, *, n_groups, epg, final_norm):
    t, d = x.shape
    n_pair = epg * (epg - 1) // 2
    n_buckets = n_groups * n_pair
    assert n_buckets <= ROUTE_ROWS and n_groups * (epg + 1) <= ROUTE_ROWS
    rt = EXPERT_TILE_ROWS
    n_tiles = t // rt + n_buckets + 1
    xa, info, cnt = _route(x, nf, wrt, brt, n_groups=n_groups, epg=epg, tm=512)
    bucket = info[:, 2, :].reshape(t).astype(jnp.int32)
    rank = info[:, 3, :].reshape(t).astype(jnp.int32)
    counts = cnt[:n_buckets, 0].astype(jnp.int32)
    tiles_per = (counts + rt - 1) // rt
    tile_end = jnp.cumsum(tiles_per)
    slot = ((tile_end - tiles_per) * rt)[bucket] + rank
    tok = jnp.zeros((n_tiles * rt,), jnp.int32).at[slot].set(jnp.arange(t, dtype=jnp.int32))
    tile_id = jnp.arange(n_tiles, dtype=jnp.int32)
    tile_bucket = jnp.minimum(jnp.searchsorted(tile_end, tile_id, side="right"), n_buckets - 1)
    pair_lo = jnp.array([a for a in range(epg) for b in range(a + 1, epg)], jnp.int32)
    pair_hi = jnp.array([b for a in range(epg) for b in range(a + 1, epg)], jnp.int32)
    tile_grp = tile_bucket // n_pair
    tile_ea = (tile_grp * epg + pair_lo[tile_bucket % n_pair]).astype(jnp.int32)
    tile_eb = (tile_grp * epg + pair_hi[tile_bucket % n_pair]).astype(jnp.int32)
    ys = _experts(tok, tile_ea, tile_eb, tile_end[-1:].astype(jnp.int32), xa, nf, fin, wg, wu, wd,
                  final_norm=final_norm)
    return ys, slot


def _kvq_prompt_kernel(slot_ref, ys_ref, nkv_ref, nq_ref, wk_ref, wv_ref, wq_ref,
                       x_ref, kt_ref, v3_ref, kb_ref, vbt_ref, qbt_ref, xbuf, sem, *, q_scale, page):
    i = pl.program_id(0)
    n_steps = pl.num_programs(0)
    tm, d = x_ref.shape
    n_heads, dv = v3_ref.shape[1:]
    sl = i % 2

    @pl.when(i == 0)
    def _():
        _gather_rows(ys_ref, slot_ref, 0, xbuf.at[0], sem.at[0], tm, False)

    _gather_rows(ys_ref, slot_ref, i * tm, xbuf.at[sl], sem.at[sl], tm, True)
    nxt = jnp.minimum(i + 1, n_steps - 1)
    _gather_rows(ys_ref, slot_ref, nxt * tm, xbuf.at[1 - sl], sem.at[1 - sl], tm, False)
    x = xbuf[sl]
    x_ref[...] = x
    hk = _rms(x, nkv_ref[...]).astype(BF16)
    k = _dot(hk, wk_ref[...])
    v = _dot(hk, wv_ref[...])
    kb_ref[...] = k.astype(BF16)
    for pg in range(tm // page):
        kt_ref[pg * d:(pg + 1) * d, :] = k[pg * page:(pg + 1) * page, :].T
    for h in range(n_heads):
        v3_ref[:, h, :] = v[:, h * dv:(h + 1) * dv]
    hq = _rms(x, nq_ref[...]).astype(BF16)
    q = _dot(hq, wq_ref[...]) * q_scale
    vbt_ref[0] = v.T.astype(BF16)
    qbt_ref[0] = q.T.astype(BF16)

    @pl.when(i == n_steps - 1)
    def _():
        _gather_rows(ys_ref, slot_ref, nxt * tm, xbuf.at[1 - sl], sem.at[1 - sl], tm, True)


def _kvq_prompt(slot, ys, nkv, nq, wk, wv, wq, *, bsz, seq, n_heads, q_scale, page, tm):
    t = bsz * seq
    d = ys.shape[1]
    dv = d // n_heads
    tm = min(tm, seq)
    nt = seq // tm
    row = pl.BlockSpec((tm, d), lambda i, slot: (i, 0))
    tspec = pl.BlockSpec((1, d, tm), lambda i, slot: (i // nt, 0, i % nt))
    const = lambda a: pl.BlockSpec(a.shape, lambda i, slot: (0,) * a.ndim)
    kern = functools.partial(_kvq_prompt_kernel, q_scale=q_scale, page=page)
    return pl.pallas_call(
        kern,
        grid_spec=pltpu.PrefetchScalarGridSpec(
            num_scalar_prefetch=1,
            grid=(t // tm,),
            in_specs=[pl.BlockSpec(memory_space=pl.ANY), const(nkv), const(nq),
                      const(wk), const(wv), const(wq)],
            out_specs=[row,
                       pl.BlockSpec((tm // page * d, page), lambda i, slot: (i, 0)),
                       pl.BlockSpec((tm, n_heads, dv), lambda i, slot: (i, 0, 0)),
                       row, tspec, tspec],
            scratch_shapes=[pltpu.VMEM((2, tm, d), F32), pltpu.SemaphoreType.DMA((2,))],
        ),
        out_shape=[jax.ShapeDtypeStruct((t, d), F32),
                   jax.ShapeDtypeStruct((t // page * d, page), F32),
                   jax.ShapeDtypeStruct((t, n_heads, dv), F32),
                   jax.ShapeDtypeStruct((t, d), BF16),
                   jax.ShapeDtypeStruct((bsz, d, seq), BF16),
                   jax.ShapeDtypeStruct((bsz, d, seq), BF16)],
        compiler_params=_params(("arbitrary",)),
        name="kvq_prompt",
    )(slot, ys, nkv, nq, wk, wv, wq)


def _kvq_sample_kernel(x_ref, nkv_ref, nq_ref, wk_ref, wv_ref, wq_ref,
                       kt_ref, k_ref, v3_ref, q_ref, *, q_scale):
    n_heads, dv = v3_ref.shape[1:]
    x = x_ref[...]
    hk = _rms(x, nkv_ref[...]).astype(BF16)
    k = _dot(hk, wk_ref[...])
    v = _dot(hk, wv_ref[...])
    k_ref[...] = k
    kt_ref[...] = k.T
    for h in range(n_heads):
        v3_ref[:, h, :] = v[:, h * dv:(h + 1) * dv]
    hq = _rms(x, nq_ref[...]).astype(BF16)
    q_ref[...] = _dot(hq, wq_ref[...]) * q_scale


def _kvq_sample(x, nkv, nq, wk, wv, wq, *, n_heads, q_scale):
    t, d = x.shape
    dv = d // n_heads
    kern = functools.partial(_kvq_sample_kernel, q_scale=q_scale)
    return pl.pallas_call(
        kern,
        grid=(1,),
        in_specs=[_const_spec(x.shape), _const_spec(nkv.shape), _const_spec(nq.shape),
                  _const_spec(wk.shape), _const_spec(wv.shape), _const_spec(wq.shape)],
        out_specs=[_const_spec((d, t)), _const_spec((t, d)), _const_spec((t, n_heads, dv)),
                   _const_spec((t, d))],
        out_shape=[jax.ShapeDtypeStruct((d, t), F32), jax.ShapeDtypeStruct((t, d), F32),
                   jax.ShapeDtypeStruct((t, n_heads, dv), F32), jax.ShapeDtypeStruct((t, d), F32)],
        compiler_params=_params(("arbitrary",)),
        name="kvq_sample",
    )(x, nkv, nq, wk, wv, wq)


def _lam_value(lq_ref, lk_ref, lam_init):
    lq = lq_ref[...]
    lk = lk_ref[...]
    s = jnp.sum(lq * lk, axis=-1, keepdims=True)
    return jnp.exp(s[0:1]) - jnp.exp(s[1:2]) + lam_init


def _attn_prompt_kernel(slopes_ref, qt_ref, k_ref, kpos_ref, vt_ref, lq_ref, lk_ref, sub_ref, o_ref,
                        *, head_dim, lam_init, tq):
    h = pl.program_id(1)
    dv, seq = qt_ref.shape[1:]
    nf = kpos_ref.shape[1]
    slope = slopes_ref[h]
    lam = _lam_value(lq_ref, lk_ref, lam_init)
    row = lax.broadcasted_iota(jnp.int32, (dv, tq), 0)
    fr = lax.broadcasted_iota(jnp.int32, (SUBLANES, 2 * tq), 0)
    ci = lax.broadcasted_iota(jnp.int32, (SUBLANES, 2 * tq), 1)
    qcol = jnp.where(ci >= tq, ci - tq, ci)
    kr = lax.broadcasted_iota(jnp.int32, (tq, 2 * tq), 0)
    kc = lax.broadcasted_iota(jnp.int32, (tq, 2 * tq), 1)
    visible = jnp.where(kc >= tq, kc - tq, kc) >= kr

    for i in range(seq // tq):
        kv = (i + 1) * tq
        qt = qt_ref[0, :, i * tq:(i + 1) * tq].astype(F32)
        qq = jnp.concatenate([jnp.where(row < head_dim, qt, 0.0),
                              jnp.where(row >= head_dim, qt, 0.0)], axis=1)
        ipos = i * tq + qcol
        ihi = ((ipos // POS_SPLIT) * POS_SPLIT).astype(F32)
        ilo = (ipos % POS_SPLIT).astype(F32)
        feat = jnp.where(fr == 0, -slope * ihi,
                         jnp.where(fr == 1, -slope * ilo,
                                   jnp.where((fr == 2) | (fr == 3), slope, 0.0)))
        qqa = jnp.concatenate([qq, feat, jnp.zeros((nf - SUBLANES, 2 * tq), F32)],
                              axis=0).astype(BF16)
        ka = jnp.concatenate([k_ref[:kv, :], kpos_ref[:kv, :]], axis=1)
        s = _dot(ka, qqa)
        diag = jnp.where(visible, s[kv - tq:, :], MASK_VALUE)
        s = jnp.concatenate([s[:kv - tq, :], diag], axis=0) if i > 0 else diag
        m = jnp.max(s, axis=0, keepdims=True)
        p = jnp.exp(s - m)
        l = jnp.sum(p, axis=0, keepdims=True)
        o2 = _dot(vt_ref[0, :, :kv], p.astype(BF16)) / l
        ot = o2[:, :tq] - lam * o2[:, tq:]
        ms = jnp.mean(ot * ot, axis=0, keepdims=True)
        ot = ot * lax.rsqrt(ms + RMS_EPS)
        o_ref[i * tq:(i + 1) * tq, :] = (ot.T * sub_ref[...] * (1.0 - lam_init)).astype(o_ref.dtype)


def _attn_prompt(slopes, qbt, kb, kpos, vbt, lq, lk, sub, *, bsz, seq, n_heads, head_dim, lam_init, tq):
    _, d, _ = qbt.shape
    dv = d // n_heads
    kern = functools.partial(_attn_prompt_kernel, head_dim=head_dim, lam_init=lam_init, tq=tq)
    tspec = pl.BlockSpec((1, dv, seq), lambda b, h: (b, h, 0))
    return pl.pallas_call(
        kern,
        grid=(bsz, n_heads),
        in_specs=[pl.BlockSpec(memory_space=pltpu.SMEM),
                  tspec,
                  pl.BlockSpec((seq, dv), lambda b, h: (b, h)),
                  pl.BlockSpec(kpos.shape, lambda b, h: (0, 0)),
                  tspec,
                  pl.BlockSpec(lq.shape, lambda b, h: (0, 0)),
                  pl.BlockSpec(lk.shape, lambda b, h: (0, 0)),
                  pl.BlockSpec(sub.shape, lambda b, h: (0, 0))],
        out_specs=pl.BlockSpec((seq, dv), lambda b, h: (b, h)),
        out_shape=jax.ShapeDtypeStruct((bsz * seq, d), BF16),
        compiler_params=_params(("arbitrary", "arbitrary")),
        name="attn_prompt",
    )(slopes, qbt, kb, kpos, vbt, lq, lk, sub)


def _lane_spread(row, n):
    return jnp.broadcast_to(row, (n, row.shape[1])).T


def _sample_scores_kernel(pt_ref, q_ref, ks_ref, slope_ref, *refs, pps, n_maps, past_len):
    k_refs = refs[:pps]
    s_ref, sself_ref, qb_ref = refs[pps:]
    j = pl.program_id(1)
    d, page = qb_ref.shape
    n_heads = n_maps // 2
    hd = d // n_maps

    def map_sums(t):
        t3 = t.reshape(n_heads, 2 * hd, page)
        return jnp.concatenate([jnp.sum(t3[:, :hd, :], axis=1), jnp.sum(t3[:, hd:, :], axis=1)],
                               axis=0)

    @pl.when(j == 0)
    def _():
        qb = _lane_spread(q_ref[0], page)
        qb_ref[...] = qb
        sself_ref[0] = map_sums(qb * _lane_spread(ks_ref[0], page))

    qb = qb_ref[...]
    slope = slope_ref[...]
    lane = lax.broadcasted_iota(jnp.int32, (n_maps, page), 1)
    for i in range(pps):
        s = map_sums(k_refs[i][0] * qb)
        dist = (past_len - (j * pps + i) * page) - lane
        s_ref[0, :, i * page:(i + 1) * page] = s - slope * dist.astype(F32)


def _sample_scores(pt, q3, ks3, slope_rows, cache_kt, *, n_pages, pps):
    bd, _, d = q3.shape
    n_pool, _, page = cache_kt.shape
    n_maps = slope_rows.shape[0]
    past_len = n_pages * page
    tok = pl.BlockSpec((1, 1, d), lambda b, j, pt: (b, 0, 0))

    def page_spec(i):
        return pl.BlockSpec((1, d, page), lambda b, j, pt: (pt[b * n_pages + j * pps + i], 0, 0))

    kern = functools.partial(_sample_scores_kernel, pps=pps, n_maps=n_maps, past_len=past_len)
    return pl.pallas_call(
        kern,
        grid_spec=pltpu.PrefetchScalarGridSpec(
            num_scalar_prefetch=1,
            grid=(bd, n_pages // pps),
            in_specs=[tok, tok, pl.BlockSpec(slope_rows.shape, lambda b, j, pt: (0, 0))]
                     + [page_spec(i) for i in range(pps)],
            out_specs=[pl.BlockSpec((1, n_maps, pps * page), lambda b, j, pt: (b, 0, j)),
                       pl.BlockSpec((1, n_maps, page), lambda b, j, pt: (b, 0, 0))],
            scratch_shapes=[pltpu.VMEM((d, page), F32)],
        ),
        out_shape=[jax.ShapeDtypeStruct((bd, n_maps, past_len), F32),
                   jax.ShapeDtypeStruct((bd, n_maps, page), F32)],
        compiler_params=_params(("arbitrary", "arbitrary")),
        name="sample_scores",
    )(pt, q3, ks3, slope_rows, *([cache_kt] * pps))


def _sample_probs_kernel(s_ref, sself_ref, lq_ref, lk_ref, a_ref, aself_ref, *, n_heads, lam_init):
    lam = _lam_value(lq_ref, lk_ref, lam_init)

    def normalized(mp):
        s = s_ref[:, mp * n_heads:(mp + 1) * n_heads, :]
        ss = sself_ref[:, mp * n_heads:(mp + 1) * n_heads, :]
        m = jnp.maximum(jnp.max(s, axis=-1, keepdims=True), ss[:, :, :1])
        p = jnp.exp(s - m)
        ps = jnp.exp(ss - m)
        l = jnp.sum(p, axis=-1, keepdims=True) + ps[:, :, :1]
        return p / l, ps / l

    p0, ps0 = normalized(0)
    p1, ps1 = normalized(1)
    a_ref[...] = p0 - lam * p1
    aself_ref[...] = ps0 - lam * ps1


def _sample_probs(s, sself, lq, lk, *, n_heads, lam_init, sb):
    bd, n_maps, past = s.shape
    lanes = sself.shape[2]
    sb = math.gcd(sb, bd)
    kern = functools.partial(_sample_probs_kernel, n_heads=n_heads, lam_init=lam_init)
    return pl.pallas_call(
        kern,
        grid=(bd // sb,),
        in_specs=[pl.BlockSpec((sb, n_maps, past), lambda i: (i, 0, 0)),
                  pl.BlockSpec((sb, n_maps, lanes), lambda i: (i, 0, 0)),
                  _const_spec(lq.shape), _const_spec(lk.shape)],
        out_specs=[pl.BlockSpec((sb, n_heads, past), lambda i: (i, 0, 0)),
                   pl.BlockSpec((sb, n_heads, lanes), lambda i: (i, 0, 0))],
        out_shape=[jax.ShapeDtypeStruct((bd, n_heads, past), F32),
                   jax.ShapeDtypeStruct((bd, n_heads, lanes), F32)],
        compiler_params=_params(("arbitrary",)),
        name="sample_probs",
    )(s, sself, lq, lk)


def _sample_values_kernel(pt_ref, at_ref, aself_ref, vs_ref, sub_ref, *refs, pps, lam_init):
    v_refs = refs[:pps]
    o_ref, acc_ref = refs[pps:]
    j = pl.program_id(1)
    nj = pl.num_programs(1)
    n_heads, dv = acc_ref.shape
    rows, lanes = v_refs[0].shape[1:]
    pos_per_blk = lanes // n_heads
    blks = rows // lanes

    @pl.when(j == 0)
    def _():
        acc_ref[...] = jnp.zeros(acc_ref.shape, F32)

    acc = acc_ref[...]
    for i in range(pps):
        for blk in range(blks):
            r = at_ref[0, pl.ds((j * pps + i) * blks + blk, 1), :]
            w = _lane_spread(r, dv)
            prod = w * v_refs[i][0, blk * lanes:(blk + 1) * lanes, :]
            acc = acc + jnp.sum(prod.reshape(pos_per_blk, n_heads, dv), axis=0)
    acc_ref[...] = acc

    @pl.when(j == nj - 1)
    def _():
        o = acc + aself_ref[0] * vs_ref[0]
        ms = jnp.mean(o * o, axis=-1, keepdims=True)
        o_ref[0] = (o * lax.rsqrt(ms + RMS_EPS) * sub_ref[...] * (1.0 - lam_init)).astype(o_ref.dtype)


def _sample_values(pt, at, aself, vs3, sub, cache_v2, *, n_pages, pps, lam_init):
    bd, n_heads, dv = vs3.shape
    n_pool, rows, lanes = cache_v2.shape
    assert lanes == dv and aself.shape[2] == dv

    def page_spec(i):
        return pl.BlockSpec((1, rows, lanes), lambda b, j, pt: (pt[b * n_pages + j * pps + i], 0, 0))

    per = lambda shape: pl.BlockSpec((1,) + tuple(shape[1:]), lambda b, j, pt: (b, 0, 0))
    kern = functools.partial(_sample_values_kernel, pps=pps, lam_init=lam_init)
    return pl.pallas_call(
        kern,
        grid_spec=pltpu.PrefetchScalarGridSpec(
            num_scalar_prefetch=1,
            grid=(bd, n_pages // pps),
            in_specs=[per(at.shape), per(aself.shape), per(vs3.shape),
                      pl.BlockSpec(sub.shape, lambda b, j, pt: (0, 0))]
                     + [page_spec(i) for i in range(pps)],
            out_specs=per((bd, n_heads, dv)),
            scratch_shapes=[pltpu.VMEM((n_heads, dv), F32)],
        ),
        out_shape=jax.ShapeDtypeStruct((bd, n_heads, dv), BF16),
        compiler_params=_params(("arbitrary", "arbitrary")),
        name="sample_values",
    )(pt, at, aself, vs3, sub, *([cache_v2] * pps))


def _oproj_kernel(x_ref, a_ref, w_ref, o_ref):
    o_ref[...] = x_ref[...] + _dot(a_ref[...], w_ref[...])


def _oproj(x, a, w, *, tm):
    t, d = x.shape
    tm = min(tm, t)
    row = pl.BlockSpec((tm, d), lambda i: (i, 0))
    return pl.pallas_call(
        _oproj_kernel,
        grid=(t // tm,),
        in_specs=[row, pl.BlockSpec((tm, a.shape[1]), lambda i: (i, 0)), _const_spec(w.shape)],
        out_specs=row,
        out_shape=jax.ShapeDtypeStruct((t, d), F32),
        compiler_params=_params(("arbitrary",)),
        name="attn_oproj",
    )(x, a, w)


def kernel(x_prompt, x_sample, cache_k, cache_v, page_table, norm_mix, norm_ffn, w_uv, sgu_gain, w_sgu, b_sgu, w_gmlp_o, norm_kv, w_k, w_v, w_q, lam_q, lam_k, subln_gain, w_attn_o, w_router_group, b_router_group, w_router_expert, b_router_expert, w_e_gate, w_e_up, w_e_down, norm_final):
    bsz, seq, d = x_prompt.shape
    dec_b, dec_s, _ = x_sample.shape
    assert dec_s == 1, "sample group handles one new token per sequence"
    n_pool, page, n_heads, n_qk_maps, head_dim = cache_k.shape
    v_dim = cache_v.shape[-1]
    assert n_heads * v_dim == d and n_qk_maps * head_dim == v_dim and n_qk_maps == 2
    assert page == LANES and v_dim == LANES and n_heads == SUBLANES
    n_a = w_uv.shape[0]
    depth = norm_mix.shape[0]
    assert n_a == 1 and depth == 2, "one gMLP layer followed by one attention layer"
    _, n_sgu, chunk, _ = w_sgu.shape
    d_gm = sgu_gain.shape[1]
    sgw = d_gm // n_sgu
    n_groups = w_router_group.shape[2]
    epg = w_router_expert.shape[3]
    n_e = n_groups * epg
    n_pages = page_table.shape[1]
    past_len = n_pages * page
    assert seq % chunk == 0 and seq % page == 0 and seq <= POS_SPLIT * POS_SPLIT

    row2 = lambda a: a.reshape(1, -1)
    wuv = w_uv[0].astype(BF16)
    wo_g = w_gmlp_o[0].astype(BF16)
    wk = w_k.astype(BF16)
    wv = w_v.astype(BF16)
    wq = w_q[0].astype(BF16)
    wao = w_attn_o[0].astype(BF16)
    wg = w_e_gate.astype(BF16)
    wu = w_e_up.astype(BF16)
    wd = w_e_down.astype(BF16)
    wr = jnp.concatenate([jnp.transpose(w_router_expert, (0, 2, 1, 3)).reshape(depth, d, n_e),
                          w_router_group], axis=2)
    br = jnp.concatenate([b_router_expert.reshape(depth, 1, n_e),
                          b_router_group.reshape(depth, 1, n_groups)], axis=2)
    pad_r = ROUTE_ROWS - (n_e + n_groups)
    wrt = jnp.pad(jnp.transpose(wr, (0, 2, 1)), ((0, 0), (0, pad_r), (0, 0)))
    brt = jnp.broadcast_to(jnp.pad(jnp.transpose(br, (0, 2, 1)), ((0, 0), (0, pad_r), (0, 0))),
                           (depth, ROUTE_ROWS, LANES))
    fin = row2(norm_final)
    slopes = 2.0 ** (-8.0 * jnp.arange(1, n_heads + 1, dtype=F32) / n_heads)
    lam_init = 0.8 - 0.6 * math.exp(-0.3 * 1)
    q_scale = head_dim ** -0.5
    ws_row = jnp.repeat(w_sgu[0, :, 0, 0], sgw).reshape(1, d_gm)
    bs_row = jnp.repeat(b_sgu[0, :, 0], sgw).reshape(1, d_gm)
    bs_t = b_sgu[0].T
    sub = row2(subln_gain[0])
    kp = jnp.arange(seq, dtype=jnp.int32)
    kpos = jnp.zeros((seq, LANES), F32)
    kpos = kpos.at[:, 0].set(1.0).at[:, 1].set(1.0)
    kpos = kpos.at[:, 2].set(((kp // POS_SPLIT) * POS_SPLIT).astype(F32))
    kpos = kpos.at[:, 3].set((kp % POS_SPLIT).astype(F32)).astype(BF16)

    def moe(x, i, final_norm, tm):
        return _moe_dense(x, row2(norm_ffn[i]), wr[i], br[i],
                          wg[i], wu[i], wd[i], fin, n_groups=n_groups, epg=epg,
                          final_norm=final_norm, tm=tm)

    def moe_prompt(x, i, final_norm):
        return _moe_sparse(x, row2(norm_ffn[i]), wrt[i], brt[i],
                           wg[i], wu[i], wd[i], fin, n_groups=n_groups, epg=epg,
                           final_norm=final_norm)

    tq = min(256, seq)
    xp = x_prompt.reshape(bsz * seq, d)
    (xp,) = _gmlp(xp, row2(norm_mix[0]), wuv, row2(sgu_gain[0]), w_sgu[0], bs_t, wo_g,
                  seq_one=False, chunk=chunk, n_groups=n_sgu, tm=256)
    ys0, slot0 = moe_prompt(xp, 0, False)
    xp, kt_p, v3_p, kb, vbt, qbt = _kvq_prompt(slot0, ys0, row2(norm_kv), row2(norm_mix[1]),
                                               wk, wv, wq, bsz=bsz, seq=seq, n_heads=n_heads,
                                               q_scale=q_scale, page=page, tm=512)
    att = _attn_prompt(slopes, qbt, kb, kpos, vbt, lam_q[0], lam_k[0], sub,
                       bsz=bsz, seq=seq, n_heads=n_heads, head_dim=head_dim, lam_init=lam_init,
                       tq=tq)
    xp = _oproj(xp, att, wao, tm=512)
    ys1, slot1 = moe_prompt(xp, 1, True)
    y_prompt = _collect(slot1, ys1, tm=512).reshape(bsz, seq, d)

    xs = x_sample.reshape(dec_b, d)
    xs, gv = _gmlp(xs, row2(norm_mix[0]), wuv, row2(sgu_gain[0]), ws_row, bs_row, wo_g,
                   seq_one=True, chunk=chunk, n_groups=n_sgu, tm=128)
    xs = moe(xs, 0, False, 128)
    kt_s, k_s, v3_s, q_s = _kvq_sample(xs, row2(norm_kv), row2(norm_mix[1]), wk, wv, wq,
                                       n_heads=n_heads, q_scale=q_scale)
    cache_kt = jnp.transpose(cache_k, (0, 2, 3, 4, 1)).reshape(n_pool, d, page)
    cache_v2 = cache_v.reshape(n_pool, page * n_heads, v_dim)
    pt = page_table.reshape(-1)
    pps = math.gcd(8, n_pages)
    slope_rows = jnp.tile(slopes, n_qk_maps).reshape(n_heads * n_qk_maps, 1)
    s, sself = _sample_scores(pt, q_s.reshape(dec_b, 1, d), k_s.reshape(dec_b, 1, d), slope_rows,
                              cache_kt, n_pages=n_pages, pps=pps)
    a, aself = _sample_probs(s, sself, lam_q[0], lam_k[0], n_heads=n_heads, lam_init=lam_init, sb=8)
    at = jnp.transpose(a, (0, 2, 1)).reshape(dec_b, past_len * n_heads // LANES, LANES)
    att_s = _sample_values(pt, at, aself, v3_s, sub, cache_v2, n_pages=n_pages, pps=pps,
                           lam_init=lam_init)
    xs = _oproj(xs, att_s.reshape(dec_b, d), wao, tm=128)
    y_sample = moe(xs, 1, True, 128).reshape(dec_b, 1, d)

    n_pp = seq // page
    k_prompt = jnp.transpose(kt_p.reshape(bsz, n_pp, n_heads, n_qk_maps, head_dim, page),
                             (0, 1, 5, 2, 3, 4))
    v_prompt = v3_p.reshape(bsz, n_pp, page, n_heads, v_dim)
    k_sample = jnp.transpose(kt_s.reshape(n_heads, n_qk_maps, head_dim, dec_b),
                             (3, 0, 1, 2)).reshape(dec_b, 1, n_heads, n_qk_maps, head_dim)
    v_sample = v3_s.reshape(dec_b, 1, n_heads, v_dim)
    gmlp_v_sample = gv.reshape(n_a, dec_b, 1, d_gm)
    return (y_prompt, y_sample, k_prompt, v_prompt, k_sample, v_sample, gmlp_v_sample)
```

```python
import functools
import math

import jax
import jax.numpy as jnp
from jax import lax
from jax.experimental import pallas as pl
from jax.experimental.pallas import tpu as pltpu

F32 = jnp.float32
BF16 = jnp.bfloat16

RMS_EPS = 1e-6
MASK_VALUE = -1e30
TOP_K_INNER = 2
LANES = 128
SUBLANES = 8
VMEM_LIMIT_BYTES = 56 * 1024 * 1024
POS_SPLIT = 256


def _rms(x, g):
    ms = jnp.mean(x * x, axis=-1, keepdims=True)
    return x * lax.rsqrt(ms + RMS_EPS) * g


def _gelu_tanh(x):
    c = math.sqrt(2.0 / math.pi)
    return x * (0.5 * (1.0 + jnp.tanh(c * (x + 0.044715 * (x * x * x)))))


def _dot(a, b):
    return jnp.dot(a, b, preferred_element_type=F32)


def _params(sem):
    return pltpu.CompilerParams(dimension_semantics=sem, vmem_limit_bytes=VMEM_LIMIT_BYTES)


def _const_spec(shape):
    nd = len(shape)
    return pl.BlockSpec(shape, lambda *_: (0,) * nd)


def _gmlp_kernel(x_ref, nm_ref, wuv_ref, sg_ref, ws_ref, bs_ref, wo_ref, *out_refs,
                 seq_one, chunk, n_groups, slab_groups):
    if seq_one:
        o_ref, v_ref = out_refs
    else:
        (o_ref,) = out_refs
    tm, _ = x_ref.shape
    d_gm = sg_ref.shape[1]
    sgw = d_gm // n_groups
    x = x_ref[...]
    h = _rms(x, nm_ref[...]).astype(BF16)
    v = _gelu_tanh(_dot(h, wuv_ref[:, d_gm:]))
    vn = _rms(v, sg_ref[...])
    if seq_one:
        v_ref[...] = vn
        vb = vn.astype(BF16).astype(F32)
    else:
        vb = vn.astype(BF16)
        row = lax.broadcasted_iota(jnp.int32, (chunk, chunk), 0)
        col = lax.broadcasted_iota(jnp.int32, (chunk, chunk), 1)
        causal = row >= col
    slab = slab_groups * sgw
    acc = x
    for s in range(n_groups // slab_groups):
        c0 = s * slab
        u = _gelu_tanh(_dot(h, wuv_ref[:, c0:c0 + slab]))
        if seq_one:
            gate = vb[:, c0:c0 + slab] * ws_ref[:, c0:c0 + slab] + bs_ref[:, c0:c0 + slab]
        else:
            cols = []
            for gg in range(slab_groups):
                g = s * slab_groups + gg
                w = jnp.where(causal, ws_ref[g], 0.0).astype(BF16)
                bias = jnp.broadcast_to(bs_ref[:, g:g + 1], (chunk, sgw))
                rows = []
                for c in range(tm // chunk):
                    vv = vb[c * chunk:(c + 1) * chunk, g * sgw:(g + 1) * sgw]
                    rows.append(_dot(w, vv) + bias)
                cols.append(jnp.concatenate(rows, axis=0) if len(rows) > 1 else rows[0])
            gate = jnp.concatenate(cols, axis=1) if len(cols) > 1 else cols[0]
        p = (u * gate).astype(BF16)
        acc = acc + _dot(p, wo_ref[c0:c0 + slab, :])
    o_ref[...] = acc


def _gmlp(x, nm, wuv, sg, ws, bs, wo, *, seq_one, chunk, n_groups, tm):
    t, d = x.shape
    d_gm = sg.shape[1]
    tm = min(tm, t)
    slab_groups = 2 if n_groups % 2 == 0 else 1
    out_shape = [jax.ShapeDtypeStruct((t, d), F32)]
    out_specs = [pl.BlockSpec((tm, d), lambda i: (i, 0))]
    if seq_one:
        out_shape.append(jax.ShapeDtypeStruct((t, d_gm), F32))
        out_specs.append(pl.BlockSpec((tm, d_gm), lambda i: (i, 0)))
    kern = functools.partial(_gmlp_kernel, seq_one=seq_one, chunk=chunk, n_groups=n_groups,
                             slab_groups=slab_groups)
    return pl.pallas_call(
        kern,
        grid=(t // tm,),
        in_specs=[pl.BlockSpec((tm, d), lambda i: (i, 0)),
                  _const_spec(nm.shape), _const_spec(wuv.shape), _const_spec(sg.shape),
                  _const_spec(ws.shape), _const_spec(bs.shape), _const_spec(wo.shape)],
        out_specs=out_specs,
        out_shape=out_shape,
        compiler_params=_params(("arbitrary",)),
        name="gmlp_seq1" if seq_one else "gmlp",
    )(x, nm, wuv, sg, ws, bs, wo)


def _dot_split(a, b):
    a_hi = a.astype(BF16)
    a_lo = (a - a_hi.astype(F32)).astype(BF16)
    b_hi = b.astype(BF16)
    b_lo = (b - b_hi.astype(F32)).astype(BF16)
    return _dot(a_hi, b_hi) + _dot(a_lo, b_hi) + _dot(a_hi, b_lo)


def _router_top2(hf, wr, br, n_groups, epg):
    n_e = n_groups * epg
    logits = _dot_split(hf, wr) + br
    lane = lax.broadcasted_iota(jnp.int32, logits.shape, 1)
    neg = jnp.float32(-jnp.inf)
    is_grp = lane >= n_e
    mg = jnp.max(jnp.where(is_grp, logits, neg), axis=-1, keepdims=True)
    p_grp = 1.0 / jnp.sum(jnp.where(is_grp, jnp.exp(logits - mg), 0.0), axis=-1, keepdims=True)
    grp = jnp.min(jnp.where(is_grp & (logits == mg), lane - n_e, n_groups), axis=-1, keepdims=True)
    in_grp = (lane >= grp * epg) & (lane < grp * epg + epg)
    el1 = jnp.where(in_grp, logits, neg)
    v1 = jnp.max(el1, axis=-1, keepdims=True)
    i1 = jnp.min(jnp.where(el1 == v1, lane, n_e), axis=-1, keepdims=True)
    el2 = jnp.where(lane == i1, neg, el1)
    v2 = jnp.max(el2, axis=-1, keepdims=True)
    i2 = jnp.min(jnp.where(el2 == v2, lane, n_e), axis=-1, keepdims=True)
    e2 = jnp.exp(v2 - v1)
    w1 = 1.0 / (1.0 + e2)
    w2 = e2 / (1.0 + e2)
    return grp, i1, i2, w1 * p_grp, w2 * p_grp


def _router_gate(hf, wr, br, n_groups, epg):
    _, i1, i2, g1, g2 = _router_top2(hf, wr, br, n_groups, epg)
    lane_e = lax.broadcasted_iota(jnp.int32, (hf.shape[0], n_groups * epg), 1)
    return jnp.where(lane_e == i1, g1, 0.0) + jnp.where(lane_e == i2, g2, 0.0)


def _moe_dense_kernel(x_ref, nf_ref, wr_ref, br_ref, wg_ref, wu_ref, wd_ref,
                      fin_ref, o_ref, hb_ref, gate_ref, *, n_groups, epg, final_norm):
    e = pl.program_id(1)
    n_e = n_groups * epg

    @pl.when(e == 0)
    def _():
        x = x_ref[...]
        hf = _rms(x, nf_ref[...])
        hb_ref[...] = hf.astype(BF16)
        gate_ref[...] = _router_gate(hf, wr_ref[...], br_ref[...], n_groups, epg)
        o_ref[...] = x

    hb = hb_ref[...]
    gate = gate_ref[...]
    lane_e = lax.broadcasted_iota(jnp.int32, gate.shape, 1)
    ge = jnp.sum(jnp.where(lane_e == e, gate, 0.0), axis=-1, keepdims=True)
    a = _dot(hb, wg_ref[0])
    b = _dot(hb, wu_ref[0])
    hid = (a * jax.nn.sigmoid(a)) * b
    o_ref[...] += _dot((hid * ge).astype(BF16), wd_ref[0])

    if final_norm:
        @pl.when(e == n_e - 1)
        def _():
            o_ref[...] = _rms(o_ref[...], fin_ref[...])


def _moe_dense(x, nf, wr, br, wg, wu, wd, fin, *, n_groups, epg, final_norm, tm):
    t, d = x.shape
    n_e, _, f = wg.shape
    tm = min(tm, t)
    kern = functools.partial(_moe_dense_kernel, n_groups=n_groups, epg=epg, final_norm=final_norm)
    return pl.pallas_call(
        kern,
        grid=(t // tm, n_e),
        in_specs=[pl.BlockSpec((tm, d), lambda i, e: (i, 0)),
                  _const_spec(nf.shape), _const_spec(wr.shape), _const_spec(br.shape),
                  pl.BlockSpec((1, d, f), lambda i, e: (e, 0, 0)),
                  pl.BlockSpec((1, d, f), lambda i, e: (e, 0, 0)),
                  pl.BlockSpec((1, f, d), lambda i, e: (e, 0, 0)),
                  _const_spec(fin.shape)],
        out_specs=pl.BlockSpec((tm, d), lambda i, e: (i, 0)),
        out_shape=jax.ShapeDtypeStruct((t, d), F32),
        scratch_shapes=[pltpu.VMEM((tm, d), BF16), pltpu.VMEM((tm, n_e), F32)],
        compiler_params=_params(("arbitrary", "arbitrary")),
        name="moe_dense",
    )(x, nf, wr, br, wg, wu, wd, fin)


EXPERT_TILE_ROWS = 256
ROUTE_ROWS = 32
INFO_ROWS = 8


def _dot_nt(a, b):
    return lax.dot_general(a, b, (((1,), (1,)), ((), ())), preferred_element_type=F32)


def _route_kernel(x_ref, nf_ref, wrt_ref, brt_ref, before_ref, xa_ref, info_ref, cnt_ref, carry_ref,
                  *, n_groups, epg):
    i = pl.program_id(0)
    tm, d = x_ref.shape
    lanes = cnt_ref.shape[1]
    n_e = n_groups * epg
    n_pair = epg * (epg - 1) // 2
    rr = wrt_ref.shape[0]

    @pl.when(i == 0)
    def _():
        carry_ref[...] = jnp.zeros(carry_ref.shape, F32)

    x = x_ref[...]
    hf = _rms(x, nf_ref[...])
    h_hi = hf.astype(BF16)
    h_lo = (hf - h_hi.astype(F32)).astype(BF16)
    w = wrt_ref[...]
    w_hi = w.astype(BF16)
    w_lo = (w - w_hi.astype(F32)).astype(BF16)
    both = _dot_nt(jnp.concatenate([w_hi, w_lo], axis=0), h_hi)
    logits = both[:rr] + both[rr:] + _dot_nt(w_hi, h_lo) + brt_ref[:, :1]
    row = lax.broadcasted_iota(jnp.int32, (rr, tm), 0)
    neg = jnp.float32(-jnp.inf)
    is_grp = (row >= n_e) & (row < n_e + n_groups)
    mg = jnp.max(jnp.where(is_grp, logits, neg), axis=0, keepdims=True)
    p_grp = 1.0 / jnp.sum(jnp.where(is_grp, jnp.exp(logits - mg), 0.0), axis=0, keepdims=True)
    grp = jnp.min(jnp.where(is_grp & (logits == mg), row - n_e, n_groups), axis=0, keepdims=True)
    in_grp = (row >= grp * epg) & (row < grp * epg + epg)
    el1 = jnp.where(in_grp, logits, neg)
    v1 = jnp.max(el1, axis=0, keepdims=True)
    i1 = jnp.min(jnp.where(el1 == v1, row, n_e), axis=0, keepdims=True)
    el2 = jnp.where(row == i1, neg, el1)
    v2 = jnp.max(el2, axis=0, keepdims=True)
    i2 = jnp.min(jnp.where(el2 == v2, row, n_e), axis=0, keepdims=True)
    e2 = jnp.exp(v2 - v1)
    g1 = p_grp / (1.0 + e2)
    g2 = p_grp * (e2 / (1.0 + e2))
    first = i1 < i2
    la = jnp.where(first, i1, i2) - grp * epg
    lb = jnp.where(first, i2, i1) - grp * epg
    ga = jnp.where(first, g1, g2)
    gb = jnp.where(first, g2, g1)
    pair = (la * (2 * epg - 1 - la)) // 2 + (lb - la - 1)
    bucket = grp * n_pair + pair
    onehot = jnp.where(row == bucket, 1.0, 0.0)
    prefix = _dot(onehot.astype(BF16), before_ref[...])
    carry = carry_ref[...]
    rank = jnp.sum(onehot * (prefix + carry[:, :1]), axis=0, keepdims=True)
    carry = carry + jnp.sum(onehot, axis=1, keepdims=True)
    carry_ref[...] = carry
    cnt_ref[...] = carry
    ir = lax.broadcasted_iota(jnp.int32, (lanes, tm), 0)
    info = jnp.where(ir == 0, ga, jnp.where(ir == 1, gb,
                     jnp.where(ir == 2, bucket.astype(F32), jnp.where(ir == 3, rank, 0.0))))
    info_ref[0] = info[:INFO_ROWS]
    xa_ref[:, :d] = x
    xa_ref[:, d:] = info.T


def _route(x, nf, wrt, brt, *, n_groups, epg, tm):
    t, d = x.shape
    tm = min(tm, t)
    rr = wrt.shape[0]
    before = (jnp.arange(tm)[:, None] < jnp.arange(tm)[None, :]).astype(BF16)
    kern = functools.partial(_route_kernel, n_groups=n_groups, epg=epg)
    return pl.pallas_call(
        kern,
        grid=(t // tm,),
        in_specs=[pl.BlockSpec((tm, d), lambda i: (i, 0)),
                  _const_spec(nf.shape), _const_spec(wrt.shape), _const_spec(brt.shape),
                  _const_spec(before.shape)],
        out_specs=[pl.BlockSpec((tm, d + LANES), lambda i: (i, 0)),
                   pl.BlockSpec((1, INFO_ROWS, tm), lambda i: (i, 0, 0)),
                   _const_spec((rr, LANES))],
        out_shape=[jax.ShapeDtypeStruct((t, d + LANES), F32),
                   jax.ShapeDtypeStruct((t // tm, INFO_ROWS, tm), F32),
                   jax.ShapeDtypeStruct((rr, LANES), F32)],
        scratch_shapes=[pltpu.VMEM((rr, LANES), F32)],
        compiler_params=_params(("arbitrary",)),
        name="moe_route",
    )(x, nf, wrt, brt, before)


def _gather_rows(src_ref, idx_ref, base, dst_buf, sem, n_rows, wait):
    for r in range(n_rows):
        cp = pltpu.make_async_copy(src_ref.at[pl.ds(idx_ref[base + r], 1), :],
                                   dst_buf.at[pl.ds(r, 1), :], sem)
        if wait:
            cp.wait()
        else:
            cp.start(priority=r % 2)


def _expert_kernel(tok_ref, ea_ref, eb_ref, nv_ref, xa_ref, nf_ref, fin_ref, wga_ref, wgb_ref,
                   wua_ref, wub_ref, wda_ref, wdb_ref, ys_ref, xbuf, sem, *, final_norm):
    i = pl.program_id(0)
    n_valid = nv_ref[0]
    rt, d = ys_ref.shape
    slot = i % 3
    ahead = (i + 2) % 3

    @pl.when((i == 0) & (n_valid > 0))
    def _():
        _gather_rows(xa_ref, tok_ref, 0, xbuf.at[0], sem.at[0], rt, False)
        _gather_rows(xa_ref, tok_ref, rt, xbuf.at[1], sem.at[1], rt, False)

    @pl.when(i < n_valid)
    def _():
        _gather_rows(xa_ref, tok_ref, i * rt, xbuf.at[slot], sem.at[slot], rt, True)
        _gather_rows(xa_ref, tok_ref, (i + 2) * rt, xbuf.at[ahead], sem.at[ahead], rt, False)
        x = xbuf[slot, :, :d]
        gates = xbuf[slot, :, d:]
        h = _rms(x, nf_ref[...]).astype(BF16)

        def expert(wg_ref, wu_ref, wd_ref, gate):
            a = _dot(h, wg_ref[0])
            b = _dot(h, wu_ref[0])
            hid = (a * jax.nn.sigmoid(a)) * b
            return _dot((hid * gate).astype(BF16), wd_ref[0])

        y = (x + expert(wga_ref, wua_ref, wda_ref, gates[:, 0:1])
             + expert(wgb_ref, wub_ref, wdb_ref, gates[:, 1:2]))
        if final_norm:
            y = _rms(y, fin_ref[...])
        ys_ref[...] = y

    @pl.when(i >= n_valid)
    def _():
        ys_ref[...] = jnp.zeros(ys_ref.shape, F32)

    @pl.when((i >= n_valid) & (i < n_valid + 2) & (n_valid > 0))
    def _():
        _gather_rows(xa_ref, tok_ref, i * rt, xbuf.at[slot], sem.at[slot], rt, True)


def _experts(tok, tile_ea, tile_eb, n_valid, xa, nf, fin, wg, wu, wd, *, final_norm):
    n_e, d, f = wg.shape
    xw = xa.shape[1]
    rt = EXPERT_TILE_ROWS
    n_tiles = tok.shape[0] // rt
    up_a = pl.BlockSpec((1, d, f), lambda i, tok, ea, eb, nv: (ea[i], 0, 0))
    up_b = pl.BlockSpec((1, d, f), lambda i, tok, ea, eb, nv: (eb[i], 0, 0))
    dn_a = pl.BlockSpec((1, f, d), lambda i, tok, ea, eb, nv: (ea[i], 0, 0))
    dn_b = pl.BlockSpec((1, f, d), lambda i, tok, ea, eb, nv: (eb[i], 0, 0))
    small = lambda a: pl.BlockSpec(a.shape, lambda i, tok, ea, eb, nv: (0, 0))
    kern = functools.partial(_expert_kernel, final_norm=final_norm)
    return pl.pallas_call(
        kern,
        grid_spec=pltpu.PrefetchScalarGridSpec(
            num_scalar_prefetch=4,
            grid=(n_tiles,),
            in_specs=[pl.BlockSpec(memory_space=pl.ANY), small(nf), small(fin),
                      up_a, up_b, up_a, up_b, dn_a, dn_b],
            out_specs=pl.BlockSpec((rt, d), lambda i, tok, ea, eb, nv: (i, 0)),
            scratch_shapes=[pltpu.VMEM((3, rt, xw), F32), pltpu.SemaphoreType.DMA((3,))],
        ),
        out_shape=jax.ShapeDtypeStruct((n_tiles * rt, d), F32),
        compiler_params=_params(("arbitrary",)),
        name="moe_experts",
    )(tok, tile_ea, tile_eb, n_valid, xa, nf, fin, wg, wg, wu, wu, wd, wd)


def _collect_kernel(slot_ref, ys_ref, o_ref, sem):
    tm = o_ref.shape[0]
    base = pl.program_id(0) * tm
    _gather_rows(ys_ref, slot_ref, base, o_ref, sem, tm, False)
    _gather_rows(ys_ref, slot_ref, base, o_ref, sem, tm, True)


def _collect(slot, ys, *, tm):
    t = slot.shape[0]
    d = ys.shape[1]
    tm = min(tm, t)
    return pl.pallas_call(
        _collect_kernel,
        grid_spec=pltpu.PrefetchScalarGridSpec(
            num_scalar_prefetch=1,
            grid=(t // tm,),
            in_specs=[pl.BlockSpec(memory_space=pl.ANY)],
            out_specs=pl.BlockSpec((tm, d), lambda i, slot: (i, 0)),
            scratch_shapes=[pltpu.SemaphoreType.DMA],
        ),
        out_shape=jax.ShapeDtypeStruct((t, d), F32),
        compiler_params=_params(("arbitrary",)),
        name="moe_collect",
    )(slot, ys)


def _moe_sparse(x, nf, wrt, brt, wg, wu, wd, fin, *, n_groups, epg, final_norm):
    t, d = x.shape
    n_pair = epg * (epg - 1) // 2
    n_buckets = n_groups * n_pair
    assert n_buckets <= ROUTE_ROWS and n_groups * (epg + 1) <= ROUTE_ROWS
    rt = EXPERT_TILE_ROWS
    n_tiles = t // rt + n_buckets + 2
    xa, info, cnt = _route(x, nf, wrt, brt, n_groups=n_groups, epg=epg, tm=512)
    bucket = info[:, 2, :].reshape(t).astype(jnp.int32)
    rank = info[:, 3, :].reshape(t).astype(jnp.int32)
    counts = cnt[:n_buckets, 0].astype(jnp.int32)
    tiles_per = (counts + rt - 1) // rt
    tile_end = jnp.cumsum(tiles_per)
    slot = ((tile_end - tiles_per) * rt)[bucket] + rank
    tok = jnp.zeros((n_tiles * rt,), jnp.int32).at[slot].set(jnp.arange(t, dtype=jnp.int32))
    tile_id = jnp.arange(n_tiles, dtype=jnp.int32)
    tile_bucket = jnp.minimum(jnp.searchsorted(tile_end, tile_id, side="right"), n_buckets - 1)
    pair_lo = jnp.array([a for a in range(epg) for b in range(a + 1, epg)], jnp.int32)
    pair_hi = jnp.array([b for a in range(epg) for b in range(a + 1, epg)], jnp.int32)
    tile_grp = tile_bucket // n_pair
    tile_ea = (tile_grp * epg + pair_lo[tile_bucket % n_pair]).astype(jnp.int32)
    tile_eb = (tile_grp * epg + pair_hi[tile_bucket % n_pair]).astype(jnp.int32)
    ys = _experts(tok, tile_ea, tile_eb, tile_end[-1:].astype(jnp.int32), xa, nf, fin, wg, wu, wd,
                  final_norm=final_norm)
    return ys, slot


def _kvq_prompt_kernel(slot_ref, ys_ref, nkv_ref, nq_ref, wk_ref, wv_ref, wq_ref,
                       x_ref, kt_ref, v3_ref, kb_ref, vbt_ref, qbt_ref, xbuf, sem, *, q_scale, page):
    i = pl.program_id(0)
    n_steps = pl.num_programs(0)
    tm, d = x_ref.shape
    n_heads, dv = v3_ref.shape[1:]
    sl = i % 2

    @pl.when(i == 0)
    def _():
        _gather_rows(ys_ref, slot_ref, 0, xbuf.at[0], sem.at[0], tm, False)

    _gather_rows(ys_ref, slot_ref, i * tm, xbuf.at[sl], sem.at[sl], tm, True)
    nxt = jnp.minimum(i + 1, n_steps - 1)
    _gather_rows(ys_ref, slot_ref, nxt * tm, xbuf.at[1 - sl], sem.at[1 - sl], tm, False)
    x = xbuf[sl]
    x_ref[...] = x
    hk = _rms(x, nkv_ref[...]).astype(BF16)
    k = _dot(hk, wk_ref[...])
    v = _dot(hk, wv_ref[...])
    kb_ref[...] = k.astype(BF16)
    for pg in range(tm // page):
        kt_ref[pg * d:(pg + 1) * d, :] = k[pg * page:(pg + 1) * page, :].T
    for h in range(n_heads):
        v3_ref[:, h, :] = v[:, h * dv:(h + 1) * dv]
    hq = _rms(x, nq_ref[...]).astype(BF16)
    q = _dot(hq, wq_ref[...]) * q_scale
    vbt_ref[0] = v.T.astype(BF16)
    qbt_ref[0] = q.T.astype(BF16)

    @pl.when(i == n_steps - 1)
    def _():
        _gather_rows(ys_ref, slot_ref, nxt * tm, xbuf.at[1 - sl], sem.at[1 - sl], tm, True)


def _kvq_prompt(slot, ys, nkv, nq, wk, wv, wq, *, bsz, seq, n_heads, q_scale, page, tm):
    t = bsz * seq
    d = ys.shape[1]
    dv = d // n_heads
    tm = min(tm, seq)
    nt = seq // tm
    row = pl.BlockSpec((tm, d), lambda i, slot: (i, 0))
    tspec = pl.BlockSpec((1, d, tm), lambda i, slot: (i // nt, 0, i % nt))
    const = lambda a: pl.BlockSpec(a.shape, lambda i, slot: (0,) * a.ndim)
    kern = functools.partial(_kvq_prompt_kernel, q_scale=q_scale, page=page)
    return pl.pallas_call(
        kern,
        grid_spec=pltpu.PrefetchScalarGridSpec(
            num_scalar_prefetch=1,
            grid=(t // tm,),
            in_specs=[pl.BlockSpec(memory_space=pl.ANY), const(nkv), const(nq),
                      const(wk), const(wv), const(wq)],
            out_specs=[row,
                       pl.BlockSpec((tm // page * d, page), lambda i, slot: (i, 0)),
                       pl.BlockSpec((tm, n_heads, dv), lambda i, slot: (i, 0, 0)),
                       row, tspec, tspec],
            scratch_shapes=[pltpu.VMEM((2, tm, d), F32), pltpu.SemaphoreType.DMA((2,))],
        ),
        out_shape=[jax.ShapeDtypeStruct((t, d), F32),
                   jax.ShapeDtypeStruct((t // page * d, page), F32),
                   jax.ShapeDtypeStruct((t, n_heads, dv), F32),
                   jax.ShapeDtypeStruct((t, d), BF16),
                   jax.ShapeDtypeStruct((bsz, d, seq), BF16),
                   jax.ShapeDtypeStruct((bsz, d, seq), BF16)],
        compiler_params=_params(("arbitrary",)),
        name="kvq_prompt",
    )(slot, ys, nkv, nq, wk, wv, wq)


def _kvq_sample_kernel(x_ref, nkv_ref, nq_ref, wk_ref, wv_ref, wq_ref,
                       kt_ref, k_ref, v3_ref, q_ref, *, q_scale):
    n_heads, dv = v3_ref.shape[1:]
    x = x_ref[...]
    hk = _rms(x, nkv_ref[...]).astype(BF16)
    k = _dot(hk, wk_ref[...])
    v = _dot(hk, wv_ref[...])
    k_ref[...] = k
    kt_ref[...] = k.T
    for h in range(n_heads):
        v3_ref[:, h, :] = v[:, h * dv:(h + 1) * dv]
    hq = _rms(x, nq_ref[...]).astype(BF16)
    q_ref[...] = _dot(hq, wq_ref[...]) * q_scale


def _kvq_sample(x, nkv, nq, wk, wv, wq, *, n_heads, q_scale):
    t, d = x.shape
    dv = d // n_heads
    kern = functools.partial(_kvq_sample_kernel, q_scale=q_scale)
    return pl.pallas_call(
        kern,
        grid=(1,),
        in_specs=[_const_spec(x.shape), _const_spec(nkv.shape), _const_spec(nq.shape),
                  _const_spec(wk.shape), _const_spec(wv.shape), _const_spec(wq.shape)],
        out_specs=[_const_spec((d, t)), _const_spec((t, d)), _const_spec((t, n_heads, dv)),
                   _const_spec((t, d))],
        out_shape=[jax.ShapeDtypeStruct((d, t), F32), jax.ShapeDtypeStruct((t, d), F32),
                   jax.ShapeDtypeStruct((t, n_heads, dv), F32), jax.ShapeDtypeStruct((t, d), F32)],
        compiler_params=_params(("arbitrary",)),
        name="kvq_sample",
    )(x, nkv, nq, wk, wv, wq)


def _lam_value(lq_ref, lk_ref, lam_init):
    lq = lq_ref[...]
    lk = lk_ref[...]
    s = jnp.sum(lq * lk, axis=-1, keepdims=True)
    return jnp.exp(s[0:1]) - jnp.exp(s[1:2]) + lam_init


def _attn_prompt_kernel(slopes_ref, qt_ref, k_ref, kpos_ref, vt_ref, lq_ref, lk_ref, sub_ref, o_ref,
                        *, head_dim, lam_init, tq):
    h = pl.program_id(1)
    dv, seq = qt_ref.shape[1:]
    nf = kpos_ref.shape[1]
    slope = slopes_ref[h]
    lam = _lam_value(lq_ref, lk_ref, lam_init)
    row = lax.broadcasted_iota(jnp.int32, (dv, tq), 0)
    fr = lax.broadcasted_iota(jnp.int32, (SUBLANES, 2 * tq), 0)
    ci = lax.broadcasted_iota(jnp.int32, (SUBLANES, 2 * tq), 1)
    qcol = jnp.where(ci >= tq, ci - tq, ci)
    kr = lax.broadcasted_iota(jnp.int32, (tq, 2 * tq), 0)
    kc = lax.broadcasted_iota(jnp.int32, (tq, 2 * tq), 1)
    visible = jnp.where(kc >= tq, kc - tq, kc) >= kr

    for i in range(seq // tq):
        kv = (i + 1) * tq
        qt = qt_ref[0, :, i * tq:(i + 1) * tq].astype(F32)
        qq = jnp.concatenate([jnp.where(row < head_dim, qt, 0.0),
                              jnp.where(row >= head_dim, qt, 0.0)], axis=1)
        ipos = i * tq + qcol
        ihi = ((ipos // POS_SPLIT) * POS_SPLIT).astype(F32)
        ilo = (ipos % POS_SPLIT).astype(F32)
        feat = jnp.where(fr == 0, -slope * ihi,
                         jnp.where(fr == 1, -slope * ilo,
                                   jnp.where((fr == 2) | (fr == 3), slope, 0.0)))
        qqa = jnp.concatenate([qq, feat, jnp.zeros((nf - SUBLANES, 2 * tq), F32)],
                              axis=0).astype(BF16)
        ka = jnp.concatenate([k_ref[:kv, :], kpos_ref[:kv, :]], axis=1)
        s = _dot(ka, qqa)
        diag = jnp.where(visible, s[kv - tq:, :], MASK_VALUE)
        s = jnp.concatenate([s[:kv - tq, :], diag], axis=0) if i > 0 else diag
        m = jnp.max(s, axis=0, keepdims=True)
        p = jnp.exp(s - m)
        l = jnp.sum(p, axis=0, keepdims=True)
        o2 = _dot(vt_ref[0, :, :kv], p.astype(BF16)) / l
        ot = o2[:, :tq] - lam * o2[:, tq:]
        ms = jnp.mean(ot * ot, axis=0, keepdims=True)
        ot = ot * lax.rsqrt(ms + RMS_EPS)
        o_ref[i * tq:(i + 1) * tq, :] = (ot.T * sub_ref[...] * (1.0 - lam_init)).astype(o_ref.dtype)


def _attn_prompt(slopes, qbt, kb, kpos, vbt, lq, lk, sub, *, bsz, seq, n_heads, head_dim, lam_init, tq):
    _, d, _ = qbt.shape
    dv = d // n_heads
    kern = functools.partial(_attn_prompt_kernel, head_dim=head_dim, lam_init=lam_init, tq=tq)
    tspec = pl.BlockSpec((1, dv, seq), lambda b, h: (b, h, 0))
    return pl.pallas_call(
        kern,
        grid=(bsz, n_heads),
        in_specs=[pl.BlockSpec(memory_space=pltpu.SMEM),
                  tspec,
                  pl.BlockSpec((seq, dv), lambda b, h: (b, h)),
                  pl.BlockSpec(kpos.shape, lambda b, h: (0, 0)),
                  tspec,
                  pl.BlockSpec(lq.shape, lambda b, h: (0, 0)),
                  pl.BlockSpec(lk.shape, lambda b, h: (0, 0)),
                  pl.BlockSpec(sub.shape, lambda b, h: (0, 0))],
        out_specs=pl.BlockSpec((seq, dv), lambda b, h: (b, h)),
        out_shape=jax.ShapeDtypeStruct((bsz * seq, d), BF16),
        compiler_params=_params(("arbitrary", "arbitrary")),
        name="attn_prompt",
    )(slopes, qbt, kb, kpos, vbt, lq, lk, sub)


def _lane_spread(row, n):
    return jnp.broadcast_to(row, (n, row.shape[1])).T


def _sample_scores_kernel(pt_ref, q_ref, ks_ref, slope_ref, *refs, pps, n_maps, past_len):
    k_refs = refs[:pps]
    s_ref, sself_ref, qb_ref = refs[pps:]
    j = pl.program_id(1)
    d, page = qb_ref.shape
    n_heads = n_maps // 2
    hd = d // n_maps

    def map_sums(t):
        t3 = t.reshape(n_heads, 2 * hd, page)
        return jnp.concatenate([jnp.sum(t3[:, :hd, :], axis=1), jnp.sum(t3[:, hd:, :], axis=1)],
                               axis=0)

    @pl.when(j == 0)
    def _():
        qb = _lane_spread(q_ref[0], page)
        qb_ref[...] = qb
        sself_ref[0] = map_sums(qb * _lane_spread(ks_ref[0], page))

    qb = qb_ref[...]
    slope = slope_ref[...]
    lane = lax.broadcasted_iota(jnp.int32, (n_maps, page), 1)
    for i in range(pps):
        s = map_sums(k_refs[i][0] * qb)
        dist = (past_len - (j * pps + i) * page) - lane
        s_ref[0, :, i * page:(i + 1) * page] = s - slope * dist.astype(F32)


def _sample_scores(pt, q3, ks3, slope_rows, cache_kt, *, n_pages, pps):
    bd, _, d = q3.shape
    n_pool, _, page = cache_kt.shape
    n_maps = slope_rows.shape[0]
    past_len = n_pages * page
    tok = pl.BlockSpec((1, 1, d), lambda b, j, pt: (b, 0, 0))

    def page_spec(i):
        return pl.BlockSpec((1, d, page), lambda b, j, pt: (pt[b * n_pages + j * pps + i], 0, 0))

    kern = functools.partial(_sample_scores_kernel, pps=pps, n_maps=n_maps, past_len=past_len)
    return pl.pallas_call(
        kern,
        grid_spec=pltpu.PrefetchScalarGridSpec(
            num_scalar_prefetch=1,
            grid=(bd, n_pages // pps),
            in_specs=[tok, tok, pl.BlockSpec(slope_rows.shape, lambda b, j, pt: (0, 0))]
                     + [page_spec(i) for i in range(pps)],
            out_specs=[pl.BlockSpec((1, n_maps, pps * page), lambda b, j, pt: (b, 0, j)),
                       pl.BlockSpec((1, n_maps, page), lambda b, j, pt: (b, 0, 0))],
            scratch_shapes=[pltpu.VMEM((d, page), F32)],
        ),
        out_shape=[jax.ShapeDtypeStruct((bd, n_maps, past_len), F32),
                   jax.ShapeDtypeStruct((bd, n_maps, page), F32)],
        compiler_params=_params(("arbitrary", "arbitrary")),
        name="sample_scores",
    )(pt, q3, ks3, slope_rows, *([cache_kt] * pps))


def _sample_probs_kernel(s_ref, sself_ref, lq_ref, lk_ref, a_ref, aself_ref, *, n_heads, lam_init):
    lam = _lam_value(lq_ref, lk_ref, lam_init)

    def normalized(mp):
        s = s_ref[:, mp * n_heads:(mp + 1) * n_heads, :]
        ss = sself_ref[:, mp * n_heads:(mp + 1) * n_heads, :]
        m = jnp.maximum(jnp.max(s, axis=-1, keepdims=True), ss[:, :, :1])
        p = jnp.exp(s - m)
        ps = jnp.exp(ss - m)
        l = jnp.sum(p, axis=-1, keepdims=True) + ps[:, :, :1]
        return p / l, ps / l

    p0, ps0 = normalized(0)
    p1, ps1 = normalized(1)
    a_ref[...] = p0 - lam * p1
    aself_ref[...] = ps0 - lam * ps1


def _sample_probs(s, sself, lq, lk, *, n_heads, lam_init, sb):
    bd, n_maps, past = s.shape
    lanes = sself.shape[2]
    sb = math.gcd(sb, bd)
    kern = functools.partial(_sample_probs_kernel, n_heads=n_heads, lam_init=lam_init)
    return pl.pallas_call(
        kern,
        grid=(bd // sb,),
        in_specs=[pl.BlockSpec((sb, n_maps, past), lambda i: (i, 0, 0)),
                  pl.BlockSpec((sb, n_maps, lanes), lambda i: (i, 0, 0)),
                  _const_spec(lq.shape), _const_spec(lk.shape)],
        out_specs=[pl.BlockSpec((sb, n_heads, past), lambda i: (i, 0, 0)),
                   pl.BlockSpec((sb, n_heads, lanes), lambda i: (i, 0, 0))],
        out_shape=[jax.ShapeDtypeStruct((bd, n_heads, past), F32),
                   jax.ShapeDtypeStruct((bd, n_heads, lanes), F32)],
        compiler_params=_params(("arbitrary",)),
        name="sample_probs",
    )(s, sself, lq, lk)


def _sample_values_kernel(pt_ref, at_ref, aself_ref, vs_ref, sub_ref, *refs, pps, lam_init):
    v_refs = refs[:pps]
    o_ref, acc_ref = refs[pps:]
    j = pl.program_id(1)
    nj = pl.num_programs(1)
    n_heads, dv = acc_ref.shape
    rows, lanes = v_refs[0].shape[1:]
    pos_per_blk = lanes // n_heads
    blks = rows // lanes

    @pl.when(j == 0)
    def _():
        acc_ref[...] = jnp.zeros(acc_ref.shape, F32)

    acc = acc_ref[...]
    for i in range(pps):
        for blk in range(blks):
            r = at_ref[0, pl.ds((j * pps + i) * blks + blk, 1), :]
            w = _lane_spread(r, dv)
            prod = w * v_refs[i][0, blk * lanes:(blk + 1) * lanes, :]
            acc = acc + jnp.sum(prod.reshape(pos_per_blk, n_heads, dv), axis=0)
    acc_ref[...] = acc

    @pl.when(j == nj - 1)
    def _():
        o = acc + aself_ref[0] * vs_ref[0]
        ms = jnp.mean(o * o, axis=-1, keepdims=True)
        o_ref[0] = (o * lax.rsqrt(ms + RMS_EPS) * sub_ref[...] * (1.0 - lam_init)).astype(o_ref.dtype)


def _sample_values(pt, at, aself, vs3, sub, cache_v2, *, n_pages, pps, lam_init):
    bd, n_heads, dv = vs3.shape
    n_pool, rows, lanes = cache_v2.shape
    assert lanes == dv and aself.shape[2] == dv

    def page_spec(i):
        return pl.BlockSpec((1, rows, lanes), lambda b, j, pt: (pt[b * n_pages + j * pps + i], 0, 0))

    per = lambda shape: pl.BlockSpec((1,) + tuple(shape[1:]), lambda b, j, pt: (b, 0, 0))
    kern = functools.partial(_sample_values_kernel, pps=pps, lam_init=lam_init)
    return pl.pallas_call(
        kern,
        grid_spec=pltpu.PrefetchScalarGridSpec(
            num_scalar_prefetch=1,
            grid=(bd, n_pages // pps),
            in_specs=[per(at.shape), per(aself.shape), per(vs3.shape),
                      pl.BlockSpec(sub.shape, lambda b, j, pt: (0, 0))]
                     + [page_spec(i) for i in range(pps)],
            out_specs=per((bd, n_heads, dv)),
            scratch_shapes=[pltpu.VMEM((n_heads, dv), F32)],
        ),
        out_shape=jax.ShapeDtypeStruct((bd, n_heads, dv), BF16),
        compiler_params=_params(("arbitrary", "arbitrary")),
        name="sample_values",
    )(pt, at, aself, vs3, sub, *([cache_v2] * pps))


def _oproj_kernel(x_ref, a_ref, w_ref, o_ref):
    o_ref[...] = x_ref[...] + _dot(a_ref[...], w_ref[...])


def _oproj(x, a, w, *, tm):
    t, d = x.shape
    tm = min(tm, t)
    row = pl.BlockSpec((tm, d), lambda i: (i, 0))
    return pl.pallas_call(
        _oproj_kernel,
        grid=(t // tm,),
        in_specs=[row, pl.BlockSpec((tm, a.shape[1]), lambda i: (i, 0)), _const_spec(w.shape)],
        out_specs=row,
        out_shape=jax.ShapeDtypeStruct((t, d), F32),
        compiler_params=_params(("arbitrary",)),
        name="attn_oproj",
    )(x, a, w)


def kernel(x_prompt, x_sample, cache_k, cache_v, page_table, norm_mix, norm_ffn, w_uv, sgu_gain, w_sgu, b_sgu, w_gmlp_o, norm_kv, w_k, w_v, w_q, lam_q, lam_k, subln_gain, w_attn_o, w_router_group, b_router_group, w_router_expert, b_router_expert, w_e_gate, w_e_up, w_e_down, norm_final):
    bsz, seq, d = x_prompt.shape
    dec_b, dec_s, _ = x_sample.shape
    assert dec_s == 1, "sample group handles one new token per sequence"
    n_pool, page, n_heads, n_qk_maps, head_dim = cache_k.shape
    v_dim = cache_v.shape[-1]
    assert n_heads * v_dim == d and n_qk_maps * head_dim == v_dim and n_qk_maps == 2
    assert page == LANES and v_dim == LANES and n_heads == SUBLANES
    n_a = w_uv.shape[0]
    depth = norm_mix.shape[0]
    assert n_a == 1 and depth == 2, "one gMLP layer followed by one attention layer"
    _, n_sgu, chunk, _ = w_sgu.shape
    d_gm = sgu_gain.shape[1]
    sgw = d_gm // n_sgu
    n_groups = w_router_group.shape[2]
    epg = w_router_expert.shape[3]
    n_e = n_groups * epg
    n_pages = page_table.shape[1]
    past_len = n_pages * page
    assert seq % chunk == 0 and seq % page == 0 and seq <= POS_SPLIT * POS_SPLIT

    row2 = lambda a: a.reshape(1, -1)
    wuv = w_uv[0].astype(BF16)
    wo_g = w_gmlp_o[0].astype(BF16)
    wk = w_k.astype(BF16)
    wv = w_v.astype(BF16)
    wq = w_q[0].astype(BF16)
    wao = w_attn_o[0].astype(BF16)
    wg = w_e_gate.astype(BF16)
    wu = w_e_up.astype(BF16)
    wd = w_e_down.astype(BF16)
    wr = jnp.concatenate([jnp.transpose(w_router_expert, (0, 2, 1, 3)).reshape(depth, d, n_e),
                          w_router_group], axis=2)
    br = jnp.concatenate([b_router_expert.reshape(depth, 1, n_e),
                          b_router_group.reshape(depth, 1, n_groups)], axis=2)
    pad_r = ROUTE_ROWS - (n_e + n_groups)
    wrt = jnp.pad(jnp.transpose(wr, (0, 2, 1)), ((0, 0), (0, pad_r), (0, 0)))
    brt = jnp.broadcast_to(jnp.pad(jnp.transpose(br, (0, 2, 1)), ((0, 0), (0, pad_r), (0, 0))),
                           (depth, ROUTE_ROWS, LANES))
    fin = row2(norm_final)
    slopes = 2.0 ** (-8.0 * jnp.arange(1, n_heads + 1, dtype=F32) / n_heads)
    lam_init = 0.8 - 0.6 * math.exp(-0.3 * 1)
    q_scale = head_dim ** -0.5
    ws_row = jnp.repeat(w_sgu[0, :, 0, 0], sgw).reshape(1, d_gm)
    bs_row = jnp.repeat(b_sgu[0, :, 0], sgw).reshape(1, d_gm)
    bs_t = b_sgu[0].T
    sub = row2(subln_gain[0])
    kp = jnp.arange(seq, dtype=jnp.int32)
    kpos = jnp.zeros((seq, LANES), F32)
    kpos = kpos.at[:, 0].set(1.0).at[:, 1].set(1.0)
    kpos = kpos.at[:, 2].set(((kp // POS_SPLIT) * POS_SPLIT).astype(F32))
    kpos = kpos.at[:, 3].set((kp % POS_SPLIT).astype(F32)).astype(BF16)

    def moe(x, i, final_norm, tm):
        return _moe_dense(x, row2(norm_ffn[i]), wr[i], br[i],
                          wg[i], wu[i], wd[i], fin, n_groups=n_groups, epg=epg,
                          final_norm=final_norm, tm=tm)

    def moe_prompt(x, i, final_norm):
        return _moe_sparse(x, row2(norm_ffn[i]), wrt[i], brt[i],
                           wg[i], wu[i], wd[i], fin, n_groups=n_groups, epg=epg,
                           final_norm=final_norm)

    tq = min(256, seq)
    xp = x_prompt.reshape(bsz * seq, d)
    (xp,) = _gmlp(xp, row2(norm_mix[0]), wuv, row2(sgu_gain[0]), w_sgu[0], bs_t, wo_g,
                  seq_one=False, chunk=chunk, n_groups=n_sgu, tm=256)
    ys0, slot0 = moe_prompt(xp, 0, False)
    xp, kt_p, v3_p, kb, vbt, qbt = _kvq_prompt(slot0, ys0, row2(norm_kv), row2(norm_mix[1]),
                                               wk, wv, wq, bsz=bsz, seq=seq, n_heads=n_heads,
                                               q_scale=q_scale, page=page, tm=512)
    att = _attn_prompt(slopes, qbt, kb, kpos, vbt, lam_q[0], lam_k[0], sub,
                       bsz=bsz, seq=seq, n_heads=n_heads, head_dim=head_dim, lam_init=lam_init,
                       tq=tq)
    xp = _oproj(xp, att, wao, tm=512)
    ys1, slot1 = moe_prompt(xp, 1, True)
    y_prompt = _collect(slot1, ys1, tm=512).reshape(bsz, seq, d)

    xs = x_sample.reshape(dec_b, d)
    xs, gv = _gmlp(xs, row2(norm_mix[0]), wuv, row2(sgu_gain[0]), ws_row, bs_row, wo_g,
                   seq_one=True, chunk=chunk, n_groups=n_sgu, tm=128)
    xs = moe(xs, 0, False, 128)
    kt_s, k_s, v3_s, q_s = _kvq_sample(xs, row2(norm_kv), row2(norm_mix[1]), wk, wv, wq,
                                       n_heads=n_heads, q_scale=q_scale)
    cache_kt = jnp.transpose(cache_k, (0, 2, 3, 4, 1)).reshape(n_pool, d, page)
    cache_v2 = cache_v.reshape(n_pool, page * n_heads, v_dim)
    pt = page_table.reshape(-1)
    pps = math.gcd(16, n_pages)
    slope_rows = jnp.tile(slopes, n_qk_maps).reshape(n_heads * n_qk_maps, 1)
    s, sself = _sample_scores(pt, q_s.reshape(dec_b, 1, d), k_s.reshape(dec_b, 1, d), slope_rows,
                              cache_kt, n_pages=n_pages, pps=pps)
    a, aself = _sample_probs(s, sself, lam_q[0], lam_k[0], n_heads=n_heads, lam_init=lam_init, sb=8)
    at = jnp.transpose(a, (0, 2, 1)).reshape(dec_b, past_len * n_heads // LANES, LANES)
    att_s = _sample_values(pt, at, aself, v3_s, sub, cache_v2, n_pages=n_pages, pps=pps,
                           lam_init=lam_init)
    xs = _oproj(xs, att_s.reshape(dec_b, d), wao, tm=128)
    y_sample = moe(xs, 1, True, 128).reshape(dec_b, 1, d)

    n_pp = seq // page
    k_prompt = jnp.transpose(kt_p.reshape(bsz, n_pp, n_heads, n_qk_maps, head_dim, page),
                             (0, 1, 5, 2, 3, 4))
    v_prompt = v3_p.reshape(bsz, n_pp, page, n_heads, v_dim)
    k_sample = jnp.transpose(kt_s.reshape(n_heads, n_qk_maps, head_dim, dec_b),
                             (3, 0, 1, 2)).reshape(dec_b, 1, n_heads, n_qk_maps, head_dim)
    v_sample = v3_s.reshape(dec_b, 1, n_heads, v_dim)
    gmlp_v_sample = gv.reshape(n_a, dec_b, 1, d_gm)
    return (y_prompt, y_sample, k_prompt, v_prompt, k_sample, v_sample, gmlp_v_sample)
```
